```python
import math
import jax
import jax.numpy as jnp
from jax import lax
import numpy as np

D_MODEL = 2048
BATCH = 1
SEQ = 16384
DEPTH = 1
DEC_BATCH = 32
DEC_SEQ = 1
PAST_LEN = 16384
PAGE_SIZE = 128

ATT_WIDTH = D_MODEL // 2
ATT_HEAD_DIM = 128
N_ATT_HEADS = ATT_WIDTH // ATT_HEAD_DIM
N_KV_HEADS = N_ATT_HEADS
MOBA_BLOCK = 256
MOBA_TOPK = 3
Q_BLOCK = 128
NUM_BUCKETS = 32
MAX_DISTANCE = 128
SSD_WIDTH = D_MODEL - ATT_WIDTH
SSD_HEAD_DIM = 64
N_SSD_HEADS = SSD_WIDTH // SSD_HEAD_DIM
N_SSD_GROUPS = 2
SSD_STATE = 128
CONV_WIDTH = 4
SSD_CHUNK = 256
CONV_DIM = SSD_WIDTH + 2 * N_SSD_GROUPS * SSD_STATE
IN_COLS = ATT_WIDTH + 2 * N_KV_HEADS * ATT_HEAD_DIM + SSD_WIDTH + CONV_DIM + N_SSD_HEADS
N_EXPERTS = 64
TOP_K = 8
N_EXPERT_GROUPS = 8
TOPK_GROUPS = 4
D_EXPERT = 512
D_SHARED = 512
ROUTED_SCALE = 2.5
MOE_BLOCK = 128
RMS_EPS = 1e-6

kernel_name = 'hymba_ssd_moba_moe_adaln_step'


def rmsnorm(x, g):
    xf = x.astype(jnp.float32)
    y = xf * lax.rsqrt(jnp.mean(xf * xf, axis=-1, keepdims=True) + RMS_EPS)
    return (y * g.astype(jnp.float32)).astype(x.dtype)


def adaln(c, w, b):
    return jnp.split(jax.nn.silu(c) @ w + b, 6, axis=-1)


def modulate(h, shift, scale):
    return h * (1 + scale[:, None]) + shift[:, None]


def rel_bucket(dist):
    d = jnp.maximum(dist, 0)
    max_exact = NUM_BUCKETS // 2
    ratio = jnp.log(jnp.maximum(d, 1).astype(jnp.float32) / max_exact) / math.log(MAX_DISTANCE / max_exact)
    large = jnp.minimum(max_exact + (ratio * (NUM_BUCKETS - max_exact)).astype(jnp.int32), NUM_BUCKETS - 1)
    return jnp.where(d < max_exact, d, large)


def swiglu(x, wg, wu, wd):
    return (jax.nn.silu(x @ wg) * (x @ wu)) @ wd


def split_proj(p):
    kv = N_KV_HEADS * ATT_HEAD_DIM
    cuts = []
    acc = 0
    for s in (ATT_WIDTH, kv, kv, SSD_WIDTH, CONV_DIM):
        acc += s
        cuts.append(acc)
    return jnp.split(p, cuts, axis=-1)


def mixer_inputs(x, shift, scale, norm_g, w_in_l):
    h = modulate(rmsnorm(x, norm_g), shift, scale)
    q, k, v, z, xbc, dt = split_proj(h @ w_in_l)
    b, L = x.shape[0], x.shape[1]
    q = q.reshape(b, L, N_ATT_HEADS, ATT_HEAD_DIM)
    k = k.reshape(b, L, N_KV_HEADS, ATT_HEAD_DIM)
    v = v.reshape(b, L, N_KV_HEADS, ATT_HEAD_DIM)
    return q, k, v, z, xbc, dt


def segsum(a):
    cs = jnp.cumsum(a, axis=-1)
    n = a.shape[-1]
    mask = jnp.tril(jnp.ones((n, n), dtype=bool))
    return jnp.where(mask, cs[..., :, None] - cs[..., None, :], -jnp.inf)


def ssd_scan(x, dt, a, bm, cm, init_state):
    bsz, seqlen, nh, hp = x.shape
    ng, ns = bm.shape[2], bm.shape[3]
    r = nh // ng
    T = min(SSD_CHUNK, seqlen)
    nc = -(-seqlen // T)
    pad = nc * T - seqlen
    padseq = lambda u: jnp.pad(u, [(0, 0), (0, pad)] + [(0, 0)] * (u.ndim - 2))
    x, dt, bm, cm = padseq(x), padseq(dt), padseq(bm), padseq(cm)
    xd = (x * dt[..., None]).reshape(bsz, nc, T, ng, r, hp)
    adt = (dt * a).reshape(bsz, nc, T, ng, r).transpose(0, 3, 4, 1, 2)
    a_cum = jnp.cumsum(adt, axis=-1)
    bc = bm.reshape(bsz, nc, T, ng, ns)
    cc = cm.reshape(bsz, nc, T, ng, ns)
    decay_in = jnp.exp(segsum(adt))
    cb = jnp.einsum('bclgn,bcsgn->bcgls', cc, bc)
    y_diag = jnp.einsum('bcgls,bgrcls,bcsgrp->bclgrp', cb, decay_in, xd)
    decay_to_end = jnp.exp(a_cum[..., -1:] - a_cum)
    chunk_states = jnp.einsum('bclgn,bgrcl,bclgrp->bcgrpn', bc, decay_to_end, xd)
    states = jnp.concatenate([init_state.reshape(bsz, 1, ng, r, hp, ns), chunk_states], axis=1)
    a_chunk = jnp.pad(a_cum[..., -1], [(0, 0), (0, 0), (0, 0), (1, 0)])
    decay_chunk = jnp.exp(segsum(a_chunk))
    carried = jnp.einsum('bgrzc,bcgrpn->bzgrpn', decay_chunk, states)
    y_off = jnp.einsum('bclgn,bcgrpn,bgrcl->bclgrp', cc, carried[:, :-1], jnp.exp(a_cum))
    y = (y_diag + y_off).reshape(bsz, nc * T, nh, hp)[:, :seqlen]
    return y, carried[:, -1].reshape(bsz, nh, hp, ns)


def ssd_mixer(z, xbc, dt_raw, conv_state, ssm_state, conv_w, conv_b, dt_bias, a_log, d_skip, ssd_norm):
    bsz, L = xbc.shape[0], xbc.shape[1]
    xpad = jnp.concatenate([conv_state.astype(xbc.dtype), xbc], axis=1)
    conv = conv_b
    for i in range(CONV_WIDTH):
        conv = conv + xpad[:, i:i + L] * conv_w[i]
    new_conv = xpad[:, xpad.shape[1] - (CONV_WIDTH - 1):]
    act = jax.nn.silu(conv).astype(jnp.float32)
    gn = N_SSD_GROUPS * SSD_STATE
    xs = act[..., :SSD_WIDTH].reshape(bsz, L, N_SSD_HEADS, SSD_HEAD_DIM)
    bm = act[..., SSD_WIDTH:SSD_WIDTH + gn].reshape(bsz, L, N_SSD_GROUPS, SSD_STATE)
    cm = act[..., SSD_WIDTH + gn:].reshape(bsz, L, N_SSD_GROUPS, SSD_STATE)
    dt = jax.nn.softplus(dt_raw.astype(jnp.float32) + dt_bias.astype(jnp.float32))
    a = -jnp.exp(a_log.astype(jnp.float32))
    y, final = ssd_scan(xs, dt, a, bm, cm, ssm_state.astype(jnp.float32))
    y = (y + d_skip.astype(jnp.float32)[:, None] * xs).reshape(bsz, L, SSD_WIDTH)
    y = rmsnorm(y * jax.nn.silu(z.astype(jnp.float32)), ssd_norm).astype(z.dtype)
    return y, final.astype(ssm_state.dtype), new_conv


def moba_prompt(q, k, v, rel_t):
    bsz, seqlen, nh, hd = q.shape
    nb = -(-seqlen // MOBA_BLOCK)
    pad = nb * MOBA_BLOCK - seqlen
    to_blocks = lambda u: jnp.pad(u, [(0, 0), (0, pad), (0, 0), (0, 0)]).reshape(bsz, nb, MOBA_BLOCK, nh, hd).transpose(0, 3, 1, 2, 4)
    kb, vb = to_blocks(k), to_blocks(v)
    k_mean = jnp.mean(kb.astype(jnp.float32), axis=3)
    q_blk = jnp.arange(seqlen) // MOBA_BLOCK
    gate = jnp.einsum('bshd,bhnd->bhsn', q.astype(jnp.float32), k_mean)
    gate = jnp.where(jnp.arange(nb)[None, :] < q_blk[:, None], gate, -jnp.inf)
    n_sel = min(MOBA_TOPK, nb)
    _, sel = lax.top_k(gate, n_sel)
    n_chunks = seqlen // Q_BLOCK
    q_c = q.transpose(0, 2, 1, 3).reshape(bsz, nh, n_chunks, Q_BLOCK, hd).transpose(2, 0, 1, 3, 4)
    sel_c = sel.reshape(bsz, nh, n_chunks, Q_BLOCK, n_sel).transpose(2, 0, 1, 3, 4)
    bi = jnp.arange(bsz)[:, None, None, None]
    hi = jnp.arange(nh)[None, :, None, None]
    offs = jnp.arange(MOBA_BLOCK)
    scale = hd ** -0.5

    def chunk(args):
        qq, ss, ci = args
        t = ci * Q_BLOCK + jnp.arange(Q_BLOCK)
        own = (ci * Q_BLOCK) // MOBA_BLOCK
        k_g = kb[bi, hi, ss]
        v_g = vb[bi, hi, ss]
        pos_sel = ss[..., None] * MOBA_BLOCK + offs
        s_sel = jnp.einsum('bhqd,bhqnkd->bhqnk', qq, k_g, preferred_element_type=jnp.float32) * scale
        s_sel = s_sel + rel_t[hi[..., None], rel_bucket(t[:, None, None] - pos_sel)]
        keep = (jnp.arange(n_sel) < own)[None, None, None, :, None]
        s_sel = jnp.where(keep, s_sel, -jnp.inf)
        k_own = lax.dynamic_index_in_dim(kb, own, axis=2, keepdims=False)
        v_own = lax.dynamic_index_in_dim(vb, own, axis=2, keepdims=False)
        d_own = t[:, None] - (own * MOBA_BLOCK + offs)[None, :]
        s_own = jnp.einsum('bhqd,bhkd->bhqk', qq, k_own, preferred_element_type=jnp.float32) * scale
        s_own = jnp.where(d_own >= 0, s_own + rel_t[:, rel_bucket(d_own)], -jnp.inf)
        n_k = n_sel * MOBA_BLOCK
        p = jax.nn.softmax(jnp.concatenate([s_sel.reshape(bsz, nh, Q_BLOCK, n_k), s_own], axis=-1), axis=-1).astype(v.dtype)
        p_sel = p[..., :n_k].reshape(bsz, nh, Q_BLOCK, n_sel, MOBA_BLOCK)
        return jnp.einsum('bhqnk,bhqnkd->bhqd', p_sel, v_g) + jnp.einsum('bhqk,bhkd->bhqd', p[..., n_k:], v_own)

    out = lax.map(chunk, (q_c, sel_c, jnp.arange(n_chunks)))
    return out.transpose(1, 0, 3, 2, 4).reshape(bsz, seqlen, nh, hd)


def moba_sample(q, k, v, k_pool, v_pool, page_table, rel_t):
    bsz, n_new, nh, hd = q.shape
    n_pages = page_table.shape[1]
    past_len = n_pages * PAGE_SIZE
    ppb = MOBA_BLOCK // PAGE_SIZE
    np_new = -(-n_new // PAGE_SIZE)
    to_pages = lambda u: jnp.pad(u, [(0, 0), (0, np_new * PAGE_SIZE - n_new), (0, 0), (0, 0)]).reshape(bsz, np_new, PAGE_SIZE, nh, hd)
    k_new_pg, v_new_pg = to_pages(k), to_pages(v)
    page_sum = jnp.concatenate([jnp.sum(k_pool.astype(jnp.float32), axis=1)[page_table],
                                jnp.sum(k_new_pg.astype(jnp.float32), axis=2)], axis=1)
    n_lp = n_pages + np_new
    nb = -(-n_lp // ppb)
    page_sum = jnp.pad(page_sum, [(0, 0), (0, nb * ppb - n_lp), (0, 0), (0, 0)])
    k_mean = page_sum.reshape(bsz, nb, ppb, nh, hd).sum(axis=2) / MOBA_BLOCK
    t = past_len + jnp.arange(n_new)
    q_blk = t // MOBA_BLOCK
    gate = jnp.einsum('bshd,bnhd->bhsn', q.astype(jnp.float32), k_mean)
    gate = jnp.where(jnp.arange(nb)[None, :] < q_blk[:, None], gate, -jnp.inf)
    n_sel = min(MOBA_TOPK, nb)
    _, sel = lax.top_k(gate, n_sel)
    blocks = jnp.concatenate([sel, jnp.broadcast_to(q_blk[None, None, :, None], (bsz, nh, n_new, 1))], axis=-1)
    lpage = blocks[..., None] * ppb + jnp.arange(ppb)
    bi = jnp.arange(bsz)[:, None, None, None, None]
    hi = jnp.arange(nh)[None, :, None, None, None]
    phys = page_table[bi, jnp.clip(lpage, 0, n_pages - 1)]
    newp = jnp.clip(lpage - n_pages, 0, np_new - 1)
    in_past = (lpage < n_pages)[..., None, None]
    shp = blocks.shape + (MOBA_BLOCK, hd)
    k_g = jnp.where(in_past, k_pool[phys, :, hi], k_new_pg[bi, newp, :, hi]).reshape(shp)
    v_g = jnp.where(in_past, v_pool[phys, :, hi], v_new_pg[bi, newp, :, hi]).reshape(shp)
    pos = blocks[..., None] * MOBA_BLOCK + jnp.arange(MOBA_BLOCK)
    dist = t[:, None, None] - pos
    slot_ok = jnp.concatenate([jnp.arange(n_sel)[None, :] < q_blk[:, None], jnp.ones((n_new, 1), dtype=bool)], axis=-1)
    keep = slot_ok[None, None, :, :, None] & (dist >= 0)
    s = jnp.einsum('bshd,bhsnkd->bhsnk', q, k_g, preferred_element_type=jnp.float32) * (hd ** -0.5)
    s = jnp.where(keep, s + rel_t[hi, rel_bucket(dist)], -jnp.inf)
    n_keys = (n_sel + 1) * MOBA_BLOCK
    p = jax.nn.softmax(s.reshape(bsz, nh, n_new, n_keys), axis=-1).astype(v.dtype).reshape(s.shape)
    return jnp.einsum('bhsnk,bhsnkd->bshd', p, v_g)


def moe_ffn(h, w_router, router_bias, wg, wu, wd, wgs, wus, wds):
    shape = h.shape
    xt = h.reshape(-1, shape[-1])
    T = xt.shape[0]
    scores = jax.nn.sigmoid(jnp.einsum('td,de->te', xt, w_router, preferred_element_type=jnp.float32))
    biased = scores + router_bias.astype(jnp.float32)
    per_group = N_EXPERTS // N_EXPERT_GROUPS
    grp_score = lax.top_k(biased.reshape(T, N_EXPERT_GROUPS, per_group), 2)[0].sum(-1)
    _, top_g = lax.top_k(grp_score, TOPK_GROUPS)
    gmask = jnp.any(top_g[..., None] == jnp.arange(N_EXPERT_GROUPS), axis=1)
    masked = jnp.where(jnp.repeat(gmask, per_group, axis=1), biased, -jnp.inf)
    _, idx = lax.top_k(masked, TOP_K)
    w = jnp.take_along_axis(scores, idx, axis=1)
    w = w / jnp.sum(w, axis=-1, keepdims=True) * ROUTED_SCALE
    n_assign = T * TOP_K
    flat_e = idx.reshape(-1)
    flat_t = jnp.arange(n_assign) // TOP_K
    order = jnp.argsort(flat_e)
    se = flat_e[order]
    counts = jnp.bincount(flat_e, length=N_EXPERTS)
    starts = jnp.cumsum(counts) - counts
    padded = (counts + MOE_BLOCK - 1) // MOE_BLOCK * MOE_BLOCK
    pends = jnp.cumsum(padded)
    dest = (pends - padded)[se] + jnp.arange(n_assign) - starts[se]
    n_rows = -(-n_assign // MOE_BLOCK) * MOE_BLOCK + N_EXPERTS * MOE_BLOCK
    n_blk = n_rows // MOE_BLOCK
    row_tok = jnp.zeros((n_rows,), jnp.int32).at[dest].set(flat_t[order])
    row_w = jnp.zeros((n_rows,), xt.dtype).at[dest].set(w.reshape(-1)[order].astype(xt.dtype))
    blk_e = jnp.minimum(jnp.searchsorted(pends, jnp.arange(n_blk) * MOE_BLOCK, side='right'), N_EXPERTS - 1)

    def run_block(args):
        tok, wb, e = args
        return swiglu(xt[tok], wg[e], wu[e], wd[e]) * wb[:, None]

    out = lax.map(run_block, (row_tok.reshape(n_blk, MOE_BLOCK), row_w.reshape(n_blk, MOE_BLOCK), blk_e))
    routed = jax.ops.segment_sum(out.reshape(n_rows, -1), row_tok, num_segments=T)
    return (routed + swiglu(xt, wgs, wus, wds)).reshape(shape)


def setup_inputs(seed: int = 0) -> dict:
    key = jax.random.key(seed)
    keys = jax.random.split(key, 40)
    counter = [0]

    def nk():
        counter[0] += 1
        return keys[counter[0] - 1]

    def nrm(shape, s):
        return jax.random.normal(nk(), shape, jnp.float32) * s

    n_pages = PAST_LEN // PAGE_SIZE
    n_pool = (5 * DEC_BATCH * n_pages + 3) // 4
    page_table = jax.random.permutation(nk(), n_pool)[:DEC_BATCH * n_pages].reshape(DEC_BATCH, n_pages).astype(jnp.int32)
    dt0 = jnp.exp(jax.random.uniform(nk(), (DEPTH, N_SSD_HEADS), minval=math.log(1e-3), maxval=math.log(1e-1)))
    dt_bias = dt0 + jnp.log(-jnp.expm1(-dt0))
    a_log = jnp.log(jax.random.uniform(nk(), (DEPTH, N_SSD_HEADS), minval=1.0, maxval=16.0))
    return {
        'x_prompt': nrm((BATCH, SEQ, D_MODEL), 1.0),
        'x_sample': nrm((DEC_BATCH, DEC_SEQ, D_MODEL), 1.0),
        'cache_k': nrm((DEPTH, n_pool, PAGE_SIZE, N_KV_HEADS, ATT_HEAD_DIM), 1.0),
        'cache_v': nrm((DEPTH, n_pool, PAGE_SIZE, N_KV_HEADS, ATT_HEAD_DIM), 1.0),
        'state_ssm': nrm((DEPTH, DEC_BATCH, N_SSD_HEADS, SSD_HEAD_DIM, SSD_STATE), 0.1),
        'state_conv': nrm((DEPTH, DEC_BATCH, CONV_WIDTH - 1, CONV_DIM), 1.0),
        'page_table': page_table,
        'c_prompt': nrm((BATCH, D_MODEL), 1.0),
        'c_sample': nrm((DEC_BATCH, D_MODEL), 1.0),
        'w_ada': nrm((DEPTH, D_MODEL, 6 * D_MODEL), 0.5 * D_MODEL ** -0.5),
        'b_ada': nrm((DEPTH, 6 * D_MODEL), 0.02),
        'norm_mix': 1.0 + nrm((DEPTH, D_MODEL), 0.02),
        'norm_ffn': 1.0 + nrm((DEPTH, D_MODEL), 0.02),
        'w_in': nrm((DEPTH, D_MODEL, IN_COLS), D_MODEL ** -0.5),
        'conv_w': nrm((DEPTH, CONV_WIDTH, CONV_DIM), CONV_WIDTH ** -0.5),
        'conv_b': nrm((DEPTH, CONV_DIM), 0.02),
        'dt_bias': dt_bias,
        'a_log': a_log,
        'd_skip': 1.0 + nrm((DEPTH, N_SSD_HEADS), 0.02),
        'ssd_norm': 1.0 + nrm((DEPTH, SSD_WIDTH), 0.02),
        'w_out': nrm((DEPTH, ATT_WIDTH + SSD_WIDTH, D_MODEL), (ATT_WIDTH + SSD_WIDTH) ** -0.5),
        'rel_bias': nrm((NUM_BUCKETS, N_ATT_HEADS), 0.5),
        'w_router': nrm((DEPTH, D_MODEL, N_EXPERTS), D_MODEL ** -0.5),
        'router_bias': nrm((DEPTH, N_EXPERTS), 0.01),
        'w_gate_e': nrm((DEPTH, N_EXPERTS, D_MODEL, D_EXPERT), D_MODEL ** -0.5),
        'w_up_e': nrm((DEPTH, N_EXPERTS, D_MODEL, D_EXPERT), D_MODEL ** -0.5),
        'w_down_e': nrm((DEPTH, N_EXPERTS, D_EXPERT, D_MODEL), D_EXPERT ** -0.5),
        'w_gate_s': nrm((DEPTH, D_MODEL, D_SHARED), D_MODEL ** -0.5),
        'w_up_s': nrm((DEPTH, D_MODEL, D_SHARED), D_MODEL ** -0.5),
        'w_down_s': nrm((DEPTH, D_SHARED, D_MODEL), D_SHARED ** -0.5),
        'norm_final': 1.0 + nrm((D_MODEL,), 0.02),
    }


def reference(x_prompt, x_sample, cache_k, cache_v, state_ssm, state_conv, page_table, c_prompt, c_sample,
              w_ada, b_ada, norm_mix, norm_ffn, w_in, conv_w, conv_b, dt_bias, a_log, d_skip, ssd_norm,
              w_out, rel_bias, w_router, router_bias, w_gate_e, w_up_e, w_down_e, w_gate_s, w_up_s, w_down_s,
              norm_final):
    rel_t = rel_bias.T
    xp, xs = x_prompt, x_sample
    bp, sp = xp.shape[0], xp.shape[1]
    bs, sn = xs.shape[0], xs.shape[1]
    kp_l, vp_l, ks_l, vs_l, hp_l, hs_l, cp_l, cs_l = [], [], [], [], [], [], [], []
    for l in range(DEPTH):
        ssd_w = (conv_w[l], conv_b[l], dt_bias[l], a_log[l], d_skip[l], ssd_norm[l])
        ffn_w = (w_router[l], router_bias[l], w_gate_e[l], w_up_e[l], w_down_e[l], w_gate_s[l], w_up_s[l], w_down_s[l])
        mp = adaln(c_prompt, w_ada[l], b_ada[l])
        ms = adaln(c_sample, w_ada[l], b_ada[l])
        q, k, v, z, xbc, dt = mixer_inputs(xp, mp[0], mp[1], norm_mix[l], w_in[l])
        att = moba_prompt(q, k, v, rel_t)
        zero_conv = jnp.zeros((bp, CONV_WIDTH - 1, CONV_DIM), xp.dtype)
        zero_ssm = jnp.zeros((bp, N_SSD_HEADS, SSD_HEAD_DIM, SSD_STATE), xp.dtype)
        ssd, ssm_new, conv_new = ssd_mixer(z, xbc, dt, zero_conv, zero_ssm, *ssd_w)
        xp = xp + mp[2][:, None] * (jnp.concatenate([att.reshape(bp, sp, ATT_WIDTH), ssd], axis=-1) @ w_out[l])
        xp = xp + mp[5][:, None] * moe_ffn(modulate(rmsnorm(xp, norm_ffn[l]), mp[3], mp[4]), *ffn_w)
        kp_l.append(k)
        vp_l.append(v)
        hp_l.append(ssm_new)
        cp_l.append(conv_new)
        q, k, v, z, xbc, dt = mixer_inputs(xs, ms[0], ms[1], norm_mix[l], w_in[l])
        att = moba_sample(q, k, v, cache_k[l], cache_v[l], page_table, rel_t)
        ssd, ssm_new, conv_new = ssd_mixer(z, xbc, dt, state_conv[l], state_ssm[l], *ssd_w)
        xs = xs + ms[2][:, None] * (jnp.concatenate([att.reshape(bs, sn, ATT_WIDTH), ssd], axis=-1) @ w_out[l])
        xs = xs + ms[5][:, None] * moe_ffn(modulate(rmsnorm(xs, norm_ffn[l]), ms[3], ms[4]), *ffn_w)
        ks_l.append(k)
        vs_l.append(v)
        hs_l.append(ssm_new)
        cs_l.append(conv_new)
    y_prompt = rmsnorm(xp, norm_final)
    y_sample = rmsnorm(xs, norm_final)
    return (y_prompt, y_sample, jnp.stack(kp_l), jnp.stack(vp_l), jnp.stack(ks_l), jnp.stack(vs_l),
            jnp.stack(hp_l), jnp.stack(hs_l), jnp.stack(cp_l), jnp.stack(cs_l))
```

```python
import functools
import math

import jax
import jax.numpy as jnp
from jax import lax
from jax.experimental import pallas as pl
from jax.experimental.pallas import tpu as pltpu

F32 = jnp.float32
BF16 = jnp.bfloat16
I32 = jnp.int32
U32 = jnp.uint32

D_MODEL = 2048
ATT_WIDTH = 1024
HEAD_DIM = 128
N_HEADS = 8
MOBA_BLOCK = 256
MOBA_TOPK = 3
Q_BLOCK = 128
NUM_BUCKETS = 32
MAX_DISTANCE = 128
SSD_WIDTH = 1024
SSD_HEAD_DIM = 64
N_SSD_HEADS = 16
N_SSD_GROUPS = 2
SSD_STATE = 128
CONV_WIDTH = 4
SSD_CHUNK = 256
CONV_DIM = SSD_WIDTH + 2 * N_SSD_GROUPS * SSD_STATE
N_EXPERTS = 64
TOP_K = 8
N_EXPERT_GROUPS = 8
TOPK_GROUPS = 4
D_EXPERT = 512
ROUTED_SCALE = 2.5
RMS_EPS = 1e-6
PAGE_SIZE = 128

LANES = 128
COL_Q, COL_K, COL_V, COL_Z, COL_X = 0, 1024, 2048, 3072, 4096
COL_B, COL_C, COL_DT = 5120, 5376, 5632
IN_COLS = 5648
IN_COLS_PAD = 6144
PROJ_TN = 512
MOE_ROWS = 256
MASK_BIG = 2.0 ** 40
NT_DIMS = (((1,), (1,)), ((), ()))


def _cparams(sem, vmem_mb):
    return pltpu.CompilerParams(dimension_semantics=sem, vmem_limit_bytes=vmem_mb * 1024 * 1024)


def _silu(x):
    return x * jax.nn.sigmoid(x)


def _norm_mod(x, g, shift, scale):
    r = lax.rsqrt(jnp.mean(x * x, axis=-1, keepdims=True) + RMS_EPS)
    return (x * r) * g * (1.0 + scale) + shift


def _split2(x):
    hi = x.astype(BF16)
    lo = (x - hi.astype(F32)).astype(BF16)
    return hi, lo


def _split3(x):
    h1 = x.astype(BF16)
    r1 = x - h1.astype(F32)
    h2 = r1.astype(BF16)
    h3 = (r1 - h2.astype(F32)).astype(BF16)
    return h1, h2, h3


def _pack_pair(lo, hi):
    lo_b = lax.bitcast_convert_type(lo.astype(BF16).astype(F32), U32) >> 16
    hi_b = lax.bitcast_convert_type(hi.astype(BF16).astype(F32), U32) & jnp.uint32(0xFFFF0000)
    return lo_b | hi_b


def _unpack_pair(u):
    lo = lax.bitcast_convert_type(u << 16, F32)
    hi = lax.bitcast_convert_type(u & jnp.uint32(0xFFFF0000), F32)
    return lo, hi


def _adaln_body(c_ref, w_ref, b_ref, o_ref):
    s = _silu(c_ref[...])
    o_ref[...] = jnp.dot(s.astype(BF16), w_ref[...].astype(BF16), preferred_element_type=F32) + b_ref[...]


def _adaln(c_all, w, b):
    r, d = c_all.shape
    n = w.shape[1]
    tn = 1024
    return pl.pallas_call(
        _adaln_body, grid=(n // tn,),
        in_specs=[pl.BlockSpec((r, d), lambda j: (0, 0)),
                  pl.BlockSpec((d, tn), lambda j: (0, j)),
                  pl.BlockSpec((1, tn), lambda j: (0, j))],
        out_specs=pl.BlockSpec((r, tn), lambda j: (0, j)),
        out_shape=jax.ShapeDtypeStruct((r, n), F32),
        compiler_params=_cparams(("arbitrary",), 40), name="adaln")(c_all, w, b)


def _inproj_body(x_ref, g_ref, sh_ref, sc_ref, w_ref, o_ref, h_scr):
    @pl.when(pl.program_id(1) == 0)
    def _():
        h_scr[...] = _norm_mod(x_ref[...], g_ref[...], sh_ref[...], sc_ref[...]).astype(BF16)

    o_ref[...] = jnp.dot(h_scr[...], w_ref[...], preferred_element_type=F32)


def _inproj(x, g, mod, w_pad, tm):
    t, d = x.shape
    r = mod.shape[0]
    n = w_pad.shape[1]
    return pl.pallas_call(
        _inproj_body, grid=(t // tm, n // PROJ_TN),
        in_specs=[pl.BlockSpec((tm, d), lambda i, j: (i, 0)),
                  pl.BlockSpec((1, d), lambda i, j: (0, 0)),
                  pl.BlockSpec((r, d), lambda i, j: (0, 0)),
                  pl.BlockSpec((r, d), lambda i, j: (0, 1)),
                  pl.BlockSpec((d, PROJ_TN), lambda i, j: (0, j))],
        out_specs=pl.BlockSpec((tm, PROJ_TN), lambda i, j: (i, j)),
        out_shape=jax.ShapeDtypeStruct((t, n), F32),
        scratch_shapes=[pltpu.VMEM((tm, d), BF16)],
        compiler_params=_cparams(("arbitrary", "arbitrary"), 48), name="inproj")(x, g, mod, mod, w_pad)


def _kvpost_body(k_ref, v_ref, ko, vo, kbo, vbo, kmo):
    k = k_ref[...]
    v = v_ref[...]
    ko[...] = k
    vo[...] = v
    kbo[...] = k.astype(BF16)
    vbo[...] = v.astype(BF16)
    for i in range(k.shape[0] // MOBA_BLOCK):
        kmo[i] = jnp.sum(k[i * MOBA_BLOCK:(i + 1) * MOBA_BLOCK], axis=0, keepdims=True) * (1.0 / MOBA_BLOCK)


def _kvpost(proj):
    t = proj.shape[0]
    r = 512
    w = ATT_WIDTH
    nb = t // MOBA_BLOCK
    return pl.pallas_call(
        _kvpost_body, grid=(t // r,),
        in_specs=[pl.BlockSpec((r, w), lambda i: (i, COL_K // w)),
                  pl.BlockSpec((r, w), lambda i: (i, COL_V // w))],
        out_specs=[pl.BlockSpec((r, w), lambda i: (i, 0)),
                   pl.BlockSpec((r, w), lambda i: (i, 0)),
                   pl.BlockSpec((r, w), lambda i: (i, 0)),
                   pl.BlockSpec((r, w), lambda i: (i, 0)),
                   pl.BlockSpec((r // MOBA_BLOCK, 1, w), lambda i: (i, 0, 0))],
        out_shape=[jax.ShapeDtypeStruct((t, w), F32), jax.ShapeDtypeStruct((t, w), F32),
                   jax.ShapeDtypeStruct((t, w), BF16), jax.ShapeDtypeStruct((t, w), BF16),
                   jax.ShapeDtypeStruct((nb, 1, w), F32)],
        compiler_params=_cparams(("arbitrary",), 40), name="kvpost")(proj, proj)


def _rel_bucket(dist):
    d = jnp.maximum(dist, 0)
    max_exact = NUM_BUCKETS // 2
    ratio = jnp.log(jnp.maximum(d, 1).astype(F32) / max_exact) / math.log(MAX_DISTANCE / max_exact)
    large = jnp.minimum(max_exact + (ratio * (NUM_BUCKETS - max_exact)).astype(I32), NUM_BUCKETS - 1)
    return jnp.where(d < max_exact, d, large)


def _prompt_bias_tables(rel_bias):
    rel_t = rel_bias.T
    far = rel_t[:, NUM_BUCKETS - 1][:, None, None, None]
    i = jnp.arange(Q_BLOCK)[None, :, None]
    half = jnp.arange(MOBA_BLOCK // Q_BLOCK)[:, None, None]
    j = jnp.arange(MOBA_BLOCK)[None, None, :]
    d_own = half * Q_BLOCK + i - j
    own = jnp.where(d_own >= 0, rel_t[:, _rel_bucket(d_own)] - far, -MASK_BIG)
    prev = rel_t[:, _rel_bucket(d_own + MOBA_BLOCK)] - far
    return own.astype(F32), prev.astype(F32)


def _moba_body(q_ref, k_ref, v_ref, km_ref, ob_ref, pb_ref, o_ref):
    c = pl.program_id(1)
    own = c // (MOBA_BLOCK // Q_BLOCK)
    qf = q_ref[...]
    q_hi, q_lo = _split2(qf)
    km_hi, km_lo = _split2(km_ref[...])
    dg = functools.partial(lax.dot_general, dimension_numbers=NT_DIMS, preferred_element_type=F32)
    gate = dg(q_hi, km_hi) + dg(q_hi, km_lo) + dg(q_lo, km_hi)
    lane = lax.broadcasted_iota(I32, gate.shape, 1)
    lane_f = lane.astype(F32)
    valid = lane < own
    g = jnp.where(valid, gate, -jnp.inf)
    sel = jnp.zeros(gate.shape, F32)
    for _ in range(MOBA_TOPK):
        m = jnp.max(g, axis=1, keepdims=True)
        idx = jnp.min(jnp.where(g == m, lane_f, 1e9), axis=1, keepdims=True)
        pick = lane_f == idx
        sel = jnp.where(pick, 1.0, sel)
        g = jnp.where(pick, -jnp.inf, g)
    sel = jnp.where(valid, sel, 0.0)
    mq = jnp.where(sel > 0.0, 0.0, -MASK_BIG).astype(BF16)
    qb = (qf * (HEAD_DIM ** -0.5)).astype(BF16)
    q_aug = jnp.concatenate([qb, mq], axis=1)

    def kv(n):
        st = pl.multiple_of(n * MOBA_BLOCK, MOBA_BLOCK)
        return k_ref[pl.ds(st, MOBA_BLOCK), :], v_ref[pl.ds(st, MOBA_BLOCK), :]

    k_own, v_own = kv(own)
    s = dg(qb, k_own) + ob_ref[0, 0]
    m0 = jnp.max(s, axis=1, keepdims=True)
    p = jnp.exp(s - m0)
    l0 = jnp.sum(p, axis=1, keepdims=True)
    acc0 = jnp.dot(p.astype(BF16), v_own, preferred_element_type=F32)
    kcol = lax.broadcasted_iota(I32, (MOBA_BLOCK, LANES), 1)

    def block(n, carry, bias):
        m, l, acc = carry
        kn, vn = kv(n)
        onehot = jnp.where(kcol == n, 1.0, 0.0).astype(BF16)
        s = dg(q_aug, jnp.concatenate([kn, onehot], axis=1))
        if bias is not None:
            s = s + bias
        m_new = jnp.maximum(m, jnp.max(s, axis=1, keepdims=True))
        p = jnp.exp(s - m_new)
        alpha = jnp.exp(m - m_new)
        l = alpha * l + jnp.sum(p, axis=1, keepdims=True)
        acc = alpha * acc + jnp.dot(p.astype(BF16), vn, preferred_element_type=F32)
        return m_new, l, acc

    carry = (m0, l0, acc0)
    carry = lax.fori_loop(0, jnp.minimum(own, 1), lambda i, cr: block(own - 1, cr, pb_ref[0, 0]), carry)
    carry = lax.fori_loop(0, jnp.maximum(own - 1, 0), lambda n, cr: block(n, cr, None), carry)
    _, l, acc = carry
    o_ref[...] = (acc * (1.0 / l)).astype(o_ref.dtype)


def _moba_prompt(proj, kb, vb, kmean_pad, own_bias, prev_bias):
    t = proj.shape[0]
    hd = HEAD_DIM
    return pl.pallas_call(
        _moba_body, grid=(N_HEADS, t // Q_BLOCK),
        in_specs=[pl.BlockSpec((Q_BLOCK, hd), lambda h, c: (c, h)),
                  pl.BlockSpec((t, hd), lambda h, c: (0, h)),
                  pl.BlockSpec((t, hd), lambda h, c: (0, h)),
                  pl.BlockSpec((LANES, hd), lambda h, c: (0, h)),
                  pl.BlockSpec((1, 1, Q_BLOCK, MOBA_BLOCK), lambda h, c: (h, c % 2, 0, 0)),
                  pl.BlockSpec((1, 1, Q_BLOCK, MOBA_BLOCK), lambda h, c: (h, c % 2, 0, 0))],
        out_specs=pl.BlockSpec((Q_BLOCK, hd), lambda h, c: (c, h)),
        out_shape=jax.ShapeDtypeStruct((t, ATT_WIDTH), BF16),
        compiler_params=_cparams(("arbitrary", "arbitrary"), 40), name="moba_prompt")(
            proj, kb, vb, kmean_pad, own_bias, prev_bias)


def _ssd_body(z_ref, x_ref, b_ref, c_ref, dt_ref, cw_ref, cb_ref, dtb_ref, alog_ref, dsk_ref, nrm_ref,
              exp_ref, expt_ref, y_ref, st_ref, tail_ref, xpad, state):
    i = pl.program_id(0)
    L = SSD_CHUNK
    W = SSD_WIDTH
    GN = N_SSD_GROUPS * SSD_STATE

    @pl.when(i == 0)
    def _():
        xpad[0:8, :] = jnp.zeros((8, CONV_DIM), F32)
        state[...] = jnp.zeros(state.shape, F32)

    xpad[8:8 + L, 0:W] = x_ref[...]
    xpad[8:8 + L, W:W + GN] = b_ref[...]
    xpad[8:8 + L, W + GN:W + 2 * GN] = c_ref[...]
    conv = cb_ref[...] + xpad[5:5 + L, :] * cw_ref[0:1, :]
    for r in range(1, CONV_WIDTH):
        conv = conv + xpad[5 + r:5 + r + L, :] * cw_ref[r:r + 1, :]
    tail = xpad[L:L + 8, :]
    tail_ref[...] = tail
    xpad[0:8, :] = tail
    act = _silu(conv)
    xs = act[:, 0:W]
    bm = act[:, W:W + GN].astype(BF16)
    cm = act[:, W + GN:W + 2 * GN].astype(BF16)

    dtv = dt_ref[...] + dtb_ref[...]
    dt = jnp.maximum(dtv, 0.0) + jnp.log1p(jnp.exp(-jnp.abs(dtv)))
    a = -jnp.exp(alog_ref[...])
    adt = dt * a
    row = lax.broadcasted_iota(I32, (L, L), 0)
    col = lax.broadcasted_iota(I32, (L, L), 1)
    causal = row >= col
    tril = jnp.where(causal, 1.0, 0.0).astype(BF16)
    dotf = functools.partial(jnp.dot, preferred_element_type=F32)
    acum = sum(dotf(tril, part) for part in _split3(adt))
    acum_t = acum.T
    expand = exp_ref[...]

    def widen(v):
        return sum(dotf(part, expand) for part in _split3(v))

    xd = xs * widen(dt)
    xd_b = xd.astype(BF16)
    lane = lax.broadcasted_iota(I32, (L, LANES), 1)
    low_half = lane < SSD_HEAD_DIM
    dg = functools.partial(lax.dot_general, dimension_numbers=NT_DIMS, preferred_element_type=F32)
    hpg = N_SSD_HEADS // N_SSD_GROUPS
    y_cols = []
    cbs = [dg(cm[:, g * SSD_STATE:(g + 1) * SSD_STATE], bm[:, g * SSD_STATE:(g + 1) * SSD_STATE])
           for g in range(N_SSD_GROUPS)]
    for pair in range(N_SSD_HEADS // 2):
        cb = cbs[(2 * pair) // hpg]
        xj = xd_b[:, pair * LANES:(pair + 1) * LANES]
        acc = None
        for sub in range(2):
            h = 2 * pair + sub
            diff = acum[:, h:h + 1] - acum_t[h:h + 1, :]
            dec = jnp.exp(jnp.where(causal, diff, -jnp.inf))
            mh = (cb * dec).astype(BF16)
            xh = jnp.where(low_half if sub == 0 else jnp.logical_not(low_half), xj, jnp.zeros_like(xj))
            part = dotf(mh, xh)
            acc = part if acc is None else acc + part
        y_cols.append(acc)
    y_diag = jnp.concatenate(y_cols, axis=1)

    s_prev = state[...]
    s_b = s_prev.astype(BF16)
    half_w = W // N_SSD_GROUPS
    y_off = jnp.concatenate(
        [dg(cm[:, g * SSD_STATE:(g + 1) * SSD_STATE], s_b[g * half_w:(g + 1) * half_w, :])
         for g in range(N_SSD_GROUPS)], axis=1) * widen(jnp.exp(acum))

    dte = jnp.exp(acum[L - 1:L, :] - acum)
    xdd = xd * widen(dte)
    upd = jnp.concatenate(
        [dotf(xdd[:, g * half_w:(g + 1) * half_w].T.astype(BF16), bm[:, g * SSD_STATE:(g + 1) * SSD_STATE])
         for g in range(N_SSD_GROUPS)], axis=0)
    last = jnp.broadcast_to(acum_t[:, L - 1:L], (LANES, SSD_STATE))
    dec_rows = jnp.exp(sum(dotf(expt_ref[...], part) for part in _split3(last)))
    s_new = s_prev * dec_rows + upd
    state[...] = s_new
    st_ref[...] = s_new

    y = y_diag + y_off + dsk_ref[...] * xs
    y = y * _silu(z_ref[...])
    r = lax.rsqrt(jnp.mean(y * y, axis=-1, keepdims=True) + RMS_EPS)
    y_ref[...] = ((y * r) * nrm_ref[...]).astype(y_ref.dtype)


def _head_expand():
    h = jnp.arange(LANES)[:, None]
    ch = jnp.arange(SSD_WIDTH)[None, :]
    return jnp.where(ch // SSD_HEAD_DIM == h, 1.0, 0.0).astype(BF16)


def _ssd_prompt(proj, conv_w, conv_b, dt_bias_p, a_log_p, dskip_w, ssd_norm, expand, expand_t):
    t = proj.shape[0]
    L = SSD_CHUNK
    W = SSD_WIDTH
    GN = N_SSD_GROUPS * SSD_STATE
    const = lambda shape: pl.BlockSpec(shape, lambda i: (0, 0))
    return pl.pallas_call(
        _ssd_body, grid=(t // L,),
        in_specs=[pl.BlockSpec((L, W), lambda i: (i, COL_Z // W)),
                  pl.BlockSpec((L, W), lambda i: (i, COL_X // W)),
                  pl.BlockSpec((L, GN), lambda i: (i, COL_B // GN)),
                  pl.BlockSpec((L, GN), lambda i: (i, COL_C // GN)),
                  pl.BlockSpec((L, LANES), lambda i: (i, COL_DT // LANES)),
                  const((CONV_WIDTH, CONV_DIM)), const((1, CONV_DIM)), const((1, LANES)), const((1, LANES)),
                  const((1, W)), const((1, W)), const((LANES, W)), const((W, LANES))],
        out_specs=[pl.BlockSpec((L, W), lambda i: (i, 0)),
                   pl.BlockSpec((W, SSD_STATE), lambda i: (0, 0)),
                   pl.BlockSpec((8, CONV_DIM), lambda i: (0, 0))],
        out_shape=[jax.ShapeDtypeStruct((t, W), BF16),
                   jax.ShapeDtypeStruct((W, SSD_STATE), F32),
                   jax.ShapeDtypeStruct((8, CONV_DIM), F32)],
        scratch_shapes=[pltpu.VMEM((L + 8, CONV_DIM), F32), pltpu.VMEM((W, SSD_STATE), F32)],
        compiler_params=_cparams(("arbitrary",), 40), name="ssd_prompt")(
            proj, proj, proj, proj, proj, conv_w, conv_b, dt_bias_p, a_log_p, dskip_w, ssd_norm, expand, expand_t)


def _outproj_body(a_ref, s_ref, w_ref, x_ref, g_ref, o_ref):
    w = ATT_WIDTH
    acc = jnp.dot(a_ref[...], w_ref[0:w, :], preferred_element_type=F32)
    acc = acc + jnp.dot(s_ref[...], w_ref[w:, :], preferred_element_type=F32)
    o_ref[...] = x_ref[...] + g_ref[...] * acc


def _outproj(att, ssd, w_out_b, x, mod, tm):
    t, d = x.shape
    r = mod.shape[0]
    tn = PROJ_TN
    return pl.pallas_call(
        _outproj_body, grid=(t // tm, d // tn),
        in_specs=[pl.BlockSpec((tm, ATT_WIDTH), lambda i, j: (i, 0)),
                  pl.BlockSpec((tm, SSD_WIDTH), lambda i, j: (i, 0)),
                  pl.BlockSpec((ATT_WIDTH + SSD_WIDTH, tn), lambda i, j: (0, j)),
                  pl.BlockSpec((tm, tn), lambda i, j: (i, j)),
                  pl.BlockSpec((r, tn), lambda i, j: (0, 2 * (d // tn) + j))],
        out_specs=pl.BlockSpec((tm, tn), lambda i, j: (i, j)),
        out_shape=jax.ShapeDtypeStruct((t, d), F32),
        compiler_params=_cparams(("arbitrary", "arbitrary"), 40), name="outproj")(att, ssd, w_out_b, x, mod)


def _group_allreduce(v, lane, op):
    for s in (1, 2, 4):
        up = pltpu.roll(v, LANES - s, axis=1)
        dn = pltpu.roll(v, s, axis=1)
        v = op(v, jnp.where((lane & s) == 0, up, dn))
    return v


def _ffnpre_body(x_ref, g_ref, sh_ref, sc_ref, wrh_ref, wrl_ref, rb_ref, hp_ref, idx_ref, wt_ref, *, rows, n_real):
    h = _norm_mod(x_ref[...], g_ref[...], sh_ref[...], sc_ref[...])
    half = D_MODEL // 2
    packed = _pack_pair(h[:, :half], h[:, half:])
    if rows == hp_ref.shape[0]:
        hp_ref[...] = jnp.where(pl.program_id(0) < n_real, packed, jnp.zeros_like(packed))
    else:
        hp_ref[0:rows, :] = packed
        hp_ref[rows:, :] = jnp.zeros((hp_ref.shape[0] - rows, half), U32)
    h_hi, h_lo = _split2(h)
    dotf = functools.partial(jnp.dot, preferred_element_type=F32)
    logits = dotf(h_hi, wrh_ref[...]) + dotf(h_hi, wrl_ref[...]) + dotf(h_lo, wrh_ref[...])
    scores = jax.nn.sigmoid(logits)
    biased = scores + rb_ref[...]
    lane = lax.broadcasted_iota(I32, biased.shape, 1)
    lane_f = lane.astype(F32)
    per_group = N_EXPERTS // N_EXPERT_GROUPS
    m1 = _group_allreduce(biased, lane, jnp.maximum)
    i1 = _group_allreduce(jnp.where(biased == m1, lane_f, 1e9), lane, jnp.minimum)
    m2 = _group_allreduce(jnp.where(lane_f == i1, -jnp.inf, biased), lane, jnp.maximum)
    gs = m1 + m2
    grp = (lane // per_group) % N_EXPERT_GROUPS
    beaten = jnp.zeros(biased.shape, F32)
    for k in range(1, N_EXPERT_GROUPS):
        other = pltpu.roll(gs, per_group * k, axis=1)
        other_grp = (grp - k) % N_EXPERT_GROUPS
        wins = jnp.where(other > gs, 1.0, jnp.where(other == gs, jnp.where(other_grp < grp, 1.0, 0.0), 0.0))
        beaten = beaten + wins
    keep = jnp.where(lane < N_EXPERTS, jnp.where(beaten < TOPK_GROUPS, 1.0, 0.0), 0.0)
    masked = jnp.where(keep > 0.0, biased, -jnp.inf)
    idx_out = jnp.zeros(biased.shape, F32)
    w_out = jnp.zeros(biased.shape, F32)
    w_sum = jnp.zeros((biased.shape[0], 1), F32)
    for k in range(TOP_K):
        m = jnp.max(masked, axis=1, keepdims=True)
        ik = jnp.min(jnp.where(masked == m, lane_f, 1e9), axis=1, keepdims=True)
        pick = lane_f == ik
        wk = jnp.sum(jnp.where(pick, scores, 0.0), axis=1, keepdims=True)
        masked = jnp.where(pick, -jnp.inf, masked)
        idx_out = jnp.where(lane == k, ik, idx_out)
        w_out = jnp.where(lane == k, wk, w_out)
        w_sum = w_sum + wk
    idx_ref[...] = idx_out.astype(I32)
    wt_ref[...] = w_out / w_sum * ROUTED_SCALE


def _ffnpre(x, g, mod, wr_hi, wr_lo, rb, tm, hp_prev=None, hp_rows=None):
    t, d = x.shape
    r = mod.shape[0]
    half = d // 2
    n_real = t // tm
    steps = n_real if hp_prev is not None else hp_rows // tm
    row_map = lambda i: (jnp.minimum(i, n_real - 1), 0)
    in_specs = [pl.BlockSpec((tm, d), row_map),
                pl.BlockSpec((1, d), lambda i: (0, 0)),
                pl.BlockSpec((r, d), lambda i: (0, 3)),
                pl.BlockSpec((r, d), lambda i: (0, 4)),
                pl.BlockSpec((d, LANES), lambda i: (0, 0)),
                pl.BlockSpec((d, LANES), lambda i: (0, 0)),
                pl.BlockSpec((1, LANES), lambda i: (0, 0))]
    args = [x, g, mod, mod, wr_hi, wr_lo, rb]
    small = [pl.BlockSpec((tm, LANES), row_map), pl.BlockSpec((tm, LANES), row_map)]
    small_shape = [jax.ShapeDtypeStruct((t, LANES), I32), jax.ShapeDtypeStruct((t, LANES), F32)]
    inner = functools.partial(_ffnpre_body, rows=tm, n_real=n_real)
    if hp_prev is None:
        out_specs = [pl.BlockSpec((tm, half), lambda i: (i, 0))] + small
        out_shape = [jax.ShapeDtypeStruct((hp_rows, half), U32)] + small_shape
        body = inner
        aliases = {}
    else:
        start, blk = hp_rows
        in_specs.append(pl.BlockSpec(memory_space=pl.ANY))
        args.append(hp_prev)
        out_specs = [pl.BlockSpec((blk, half), lambda i: (start // blk, 0))] + small
        out_shape = [jax.ShapeDtypeStruct(hp_prev.shape, U32)] + small_shape
        body = lambda *refs: inner(*refs[:7], *refs[8:])
        aliases = {7: 0}
    return pl.pallas_call(
        body, grid=(steps,), in_specs=in_specs, out_specs=out_specs, out_shape=out_shape,
        input_output_aliases=aliases,
        compiler_params=_cparams(("arbitrary",), 40), name="ffnpre")(*args)


def _moe_body(blk_e_ref, nlive_ref, dst_ref, hp_ref, wg_ref, wu_ref, wd_ref, rw_ref, y8_ref,
              xbuf, ybuf, wg_b, wu_b, wd_b, gsem, ssem, *, n_assign):
    b = pl.program_id(0)
    nb = pl.num_programs(0)
    n_live = nlive_ref[0]
    slot = b % 2
    rows = MOE_ROWS

    def gather_copy(blk, r, sl):
        tok = dst_ref[blk * rows + r] >> 3
        return pltpu.make_async_copy(hp_ref.at[pl.ds(tok, 1)], xbuf.at[sl, pl.ds(r, 1)], gsem.at[sl])

    def scatter_copy(blk, r, sl):
        dst = dst_ref[blk * rows + r]
        return pltpu.make_async_copy(ybuf.at[sl, pl.ds(r, 1)], y8_ref.at[pl.ds(dst, 1)], ssem.at[sl])

    def for_rows(fn):
        def step(r, carry):
            fn(r)
            return carry
        lax.fori_loop(0, rows, step, 0, unroll=8)

    @pl.when(b == 0)
    def _():
        ybuf[1] = jnp.zeros(ybuf.shape[1:], U32)
        for k in range(2):
            fill = pltpu.make_async_copy(ybuf.at[1], y8_ref.at[pl.ds(n_assign + k * rows, rows)], ssem.at[1])
            fill.start()
            fill.wait()

    @pl.when(jnp.logical_and(b == 0, n_live > 0))
    def _():
        for_rows(lambda r: gather_copy(0, r, 0).start())

    @pl.when(b + 1 < n_live)
    def _():
        for_rows(lambda r: gather_copy(b + 1, r, 1 - slot).start())

    e = blk_e_ref[b]
    e_prev = blk_e_ref[jnp.maximum(b - 1, 0)]

    @pl.when(jnp.logical_and(b < n_live, jnp.logical_or(b == 0, e != e_prev)))
    def _():
        wg_b[...] = wg_ref[0].astype(BF16)
        wu_b[...] = wu_ref[0].astype(BF16)
        wd_b[...] = wd_ref[0].astype(BF16)

    @pl.when(jnp.logical_and(b >= 2, b - 2 < n_live))
    def _():
        for_rows(lambda r: scatter_copy(b - 2, r, slot).wait())

    @pl.when(b < n_live)
    def _():
        for_rows(lambda r: gather_copy(b, r, slot).wait())
        lo, hi = _unpack_pair(xbuf[slot])
        x = jnp.concatenate([lo, hi], axis=1).astype(BF16)
        dotf = functools.partial(jnp.dot, preferred_element_type=F32)
        act = (_silu(dotf(x, wg_b[...])) * dotf(x, wu_b[...])).astype(BF16)
        y = dotf(act, wd_b[...]) * rw_ref[...]
        half = D_MODEL // 2
        ybuf[slot] = _pack_pair(y[:, :half], y[:, half:])
        for_rows(lambda r: scatter_copy(b, r, slot).start())

    @pl.when(b == nb - 1)
    def _():
        @pl.when(jnp.logical_and(b >= 1, b - 1 < n_live))
        def _():
            for_rows(lambda r: scatter_copy(b - 1, r, 1 - slot).wait())

        @pl.when(b < n_live)
        def _():
            for_rows(lambda r: scatter_copy(b, r, slot).wait())


def _moe(blk_e, n_live, row_dst, row_w, hp, wg, wu, wd, n_assign, out_rows):
    n_blk = blk_e.shape[0]
    half = D_MODEL // 2
    grid_spec = pltpu.PrefetchScalarGridSpec(
        num_scalar_prefetch=3, grid=(n_blk,),
        in_specs=[pl.BlockSpec(memory_space=pl.ANY),
                  pl.BlockSpec((1, D_MODEL, D_EXPERT), lambda b, be, nl, dr: (be[b], 0, 0)),
                  pl.BlockSpec((1, D_MODEL, D_EXPERT), lambda b, be, nl, dr: (be[b], 0, 0)),
                  pl.BlockSpec((1, D_EXPERT, D_MODEL), lambda b, be, nl, dr: (be[b], 0, 0)),
                  pl.BlockSpec((MOE_ROWS, 1), lambda b, be, nl, dr: (b, 0))],
        out_specs=pl.BlockSpec(memory_space=pl.ANY),
        scratch_shapes=[pltpu.VMEM((2, MOE_ROWS, half), U32), pltpu.VMEM((2, MOE_ROWS, half), U32),
                        pltpu.VMEM((D_MODEL, D_EXPERT), BF16), pltpu.VMEM((D_MODEL, D_EXPERT), BF16),
                        pltpu.VMEM((D_EXPERT, D_MODEL), BF16),
                        pltpu.SemaphoreType.DMA((2,)), pltpu.SemaphoreType.DMA((2,))])
    assert out_rows == n_assign + 2 * MOE_ROWS
    return pl.pallas_call(
        functools.partial(_moe_body, n_assign=n_assign), grid_spec=grid_spec,
        out_shape=jax.ShapeDtypeStruct((out_rows, half), U32),
        compiler_params=_cparams(("arbitrary",), 48), name="moe")(blk_e, n_live, row_dst, hp, wg, wu, wd, row_w)


def _dispatch(idx, wts, n_tok):
    n_assign = n_tok * TOP_K
    flat_e = idx.reshape(-1)
    order = jnp.argsort(flat_e).astype(I32)
    counts = jnp.bincount(flat_e, length=N_EXPERTS).astype(I32)
    starts = jnp.cumsum(counts) - counts
    padded = (counts + MOE_ROWS - 1) // MOE_ROWS * MOE_ROWS
    pends = jnp.cumsum(padded)
    pstart = pends - padded
    n_blk = -(-n_assign // MOE_ROWS) + N_EXPERTS
    n_rows = n_blk * MOE_ROWS
    blk_start = jnp.arange(n_blk, dtype=I32) * MOE_ROWS
    blk_e = jnp.minimum(jnp.searchsorted(pends, blk_start, side='right'), N_EXPERTS - 1).astype(I32)
    n_live = (pends[-1] // MOE_ROWS).astype(I32).reshape(1)
    r = jnp.arange(n_rows, dtype=I32)
    e_r = blk_e[r // MOE_ROWS]
    off = r - pstart[e_r]
    valid = jnp.logical_and(off < counts[e_r], r < pends[-1])
    src = order[jnp.clip(starts[e_r] + off, 0, n_assign - 1)]
    spare = n_assign + ((r // MOE_ROWS) % 2) * MOE_ROWS + r % MOE_ROWS
    row_dst = jnp.where(valid, src, spare).astype(I32)
    row_w = jnp.where(valid, wts.reshape(-1)[src], 0.0).astype(F32).reshape(n_rows, 1)
    return blk_e, n_live, row_dst, row_w


def _combine_body(x_ref, y8_ref, hp_ref, wgs_ref, wus_ref, wds_ref, g_ref, nf_ref, o_ref):
    half = D_MODEL // 2
    lo = None
    hi = None
    for k in range(TOP_K):
        l_k, h_k = _unpack_pair(y8_ref[:, k * half:(k + 1) * half])
        lo = l_k if lo is None else lo + l_k
        hi = h_k if hi is None else hi + h_k
    routed = jnp.concatenate([lo, hi], axis=1)
    hl, hh = _unpack_pair(hp_ref[...])
    h = jnp.concatenate([hl, hh], axis=1).astype(BF16)
    dotf = functools.partial(jnp.dot, preferred_element_type=F32)
    act = (_silu(dotf(h, wgs_ref[...])) * dotf(h, wus_ref[...])).astype(BF16)
    shared = dotf(act, wds_ref[...])
    xo = x_ref[...] + g_ref[...] * (routed + shared)
    r = lax.rsqrt(jnp.mean(xo * xo, axis=-1, keepdims=True) + RMS_EPS)
    o_ref[...] = (xo * r) * nf_ref[...]


def _combine(x1, y8v, hp, wgs, wus, wds, mod, norm_final, tm, row0):
    t, d = x1.shape
    r = mod.shape[0]
    half = d // 2
    o = row0 // tm
    return pl.pallas_call(
        _combine_body, grid=(t // tm,),
        in_specs=[pl.BlockSpec((tm, d), lambda i: (i, 0)),
                  pl.BlockSpec((tm, TOP_K * half), lambda i: (i + o, 0)),
                  pl.BlockSpec((tm, half), lambda i: (i + o, 0)),
                  pl.BlockSpec((d, D_EXPERT), lambda i: (0, 0)),
                  pl.BlockSpec((d, D_EXPERT), lambda i: (0, 0)),
                  pl.BlockSpec((D_EXPERT, d), lambda i: (0, 0)),
                  pl.BlockSpec((r, d), lambda i: (0, 5)),
                  pl.BlockSpec((1, d), lambda i: (0, 0))],
        out_specs=pl.BlockSpec((tm, d), lambda i: (i, 0)),
        out_shape=jax.ShapeDtypeStruct((t, d), F32),
        compiler_params=_cparams(("arbitrary",), 48), name="combine")(
            x1, y8v, hp, wgs, wus, wds, mod, norm_final)


PAGES_PER_STEP = 16


def _pagesum_body(pt_ref, *refs):
    ins = refs[:PAGES_PER_STEP]
    o_ref = refs[PAGES_PER_STEP]
    ppb = MOBA_BLOCK // PAGE_SIZE
    for j in range(PAGES_PER_STEP // ppb):
        acc = None
        for p in range(ppb):
            s = jnp.sum(ins[j * ppb + p][0], axis=0, keepdims=True)
            acc = s if acc is None else acc + s
        o_ref[0, j:j + 1, :] = acc


def _pagesum(page_table_flat, kpool, n_batch, n_pages):
    w = kpool.shape[2]
    ppb = MOBA_BLOCK // PAGE_SIZE
    steps = n_pages // PAGES_PER_STEP

    def in_map(p):
        return lambda b, j, pt: (pt[b * n_pages + j * PAGES_PER_STEP + p], 0, 0)

    grid_spec = pltpu.PrefetchScalarGridSpec(
        num_scalar_prefetch=1, grid=(n_batch, steps),
        in_specs=[pl.BlockSpec((1, PAGE_SIZE, w), in_map(p)) for p in range(PAGES_PER_STEP)],
        out_specs=pl.BlockSpec((1, PAGES_PER_STEP // ppb, w), lambda b, j, pt: (b, j, 0)))
    return pl.pallas_call(
        _pagesum_body, grid_spec=grid_spec,
        out_shape=jax.ShapeDtypeStruct((n_batch, n_pages // ppb, w), F32),
        compiler_params=_cparams(("arbitrary", "arbitrary"), 40), name="pagesum")(
            page_table_flat, *([kpool] * PAGES_PER_STEP))


def _gate3_body(q_ref, ks_ref, sel_ref):
    prod = ks_ref[0] * q_ref[0] * (1.0 / MOBA_BLOCK)
    g = jnp.concatenate([jnp.sum(prod[:, h * HEAD_DIM:(h + 1) * HEAD_DIM], axis=1, keepdims=True)
                         for h in range(N_HEADS)], axis=1)
    rowi = lax.broadcasted_iota(I32, g.shape, 0).astype(F32)
    picks = []
    for _ in range(MOBA_TOPK):
        m = jnp.max(g, axis=0, keepdims=True)
        ik = jnp.min(jnp.where(g == m, rowi, 1e9), axis=0, keepdims=True)
        g = jnp.where(rowi == ik, -jnp.inf, g)
        picks.append(ik)
    picks.append(jnp.zeros((8 - MOBA_TOPK, N_HEADS), F32))
    sel_ref[0] = jnp.concatenate(picks, axis=0).astype(I32)


def _gate3(q3, ksum):
    n_batch, nb, w = ksum.shape
    return pl.pallas_call(
        _gate3_body, grid=(n_batch,),
        in_specs=[pl.BlockSpec((1, 1, w), lambda b: (b, 0, 0)),
                  pl.BlockSpec((1, nb, w), lambda b: (b, 0, 0))],
        out_specs=pl.BlockSpec((1, 8, N_HEADS), lambda b: (b, 0, 0)),
        out_shape=jax.ShapeDtypeStruct((n_batch, 8, N_HEADS), I32),
        compiler_params=_cparams(("arbitrary",), 32), name="gate3")(q3, ksum)


def _decode_body(phys_ref, blk_ref, q_ref, kn_ref, vn_ref, *refs, last_block):
    ppb = MOBA_BLOCK // PAGE_SIZE
    n_pg = MOBA_TOPK * ppb
    k_refs = refs[:n_pg]
    v_refs = refs[n_pg:2 * n_pg]
    tab_ref = refs[2 * n_pg]
    o_ref = refs[2 * n_pg + 1]
    b = pl.program_id(0)
    h = pl.program_id(1)
    q = q_ref[0, 0]
    scale = HEAD_DIM ** -0.5
    qb = jnp.broadcast_to(q, (8, HEAD_DIM)).astype(BF16)
    dg = functools.partial(lax.dot_general, dimension_numbers=NT_DIMS, preferred_element_type=F32)
    s_self = jnp.sum(q * kn_ref[0, 0], axis=1, keepdims=True) * scale + tab_ref[0, 2:3, 0:1]
    scores = []
    vals = []
    for s in range(MOBA_TOPK):
        kblk = jnp.concatenate([k_refs[s * ppb + p][0] for p in range(ppb)], axis=0).astype(BF16)
        vals.append(jnp.concatenate([v_refs[s * ppb + p][0] for p in range(ppb)], axis=0).astype(BF16))
        n = blk_ref[(b * N_HEADS + h) * MOBA_TOPK + s]
        bias = jnp.where(n == last_block, tab_ref[0, 0:1, :], tab_ref[0, 1:2, :])
        scores.append(dg(qb, kblk)[0:1, :] * scale + bias)
    m = s_self
    for sc in scores:
        m = jnp.maximum(m, jnp.max(sc, axis=1, keepdims=True))
    p_self = jnp.exp(s_self - m)
    l = p_self
    acc = p_self * vn_ref[0, 0]
    for sc, vblk in zip(scores, vals):
        p = jnp.exp(sc - m)
        l = l + jnp.sum(p, axis=1, keepdims=True)
        pb = jnp.broadcast_to(p, (8, MOBA_BLOCK)).astype(BF16)
        acc = acc + jnp.dot(pb, vblk, preferred_element_type=F32)[0:1, :]
    o_ref[0, 0] = acc * (1.0 / l)


def _decode(phys, blks, q4, kn4, vn4, kpool, vpool, tabs, last_block):
    n_batch = q4.shape[0]
    ppb = MOBA_BLOCK // PAGE_SIZE
    n_pg = MOBA_TOPK * ppb
    hd = HEAD_DIM

    def pg_map(i):
        return lambda b, h, ph, bl: (ph[(b * N_HEADS + h) * n_pg + i], 0, h)

    vec = pl.BlockSpec((1, 1, 1, hd), lambda b, h, ph, bl: (b, h, 0, 0))
    grid_spec = pltpu.PrefetchScalarGridSpec(
        num_scalar_prefetch=2, grid=(n_batch, N_HEADS),
        in_specs=[vec, vec, vec]
        + [pl.BlockSpec((1, PAGE_SIZE, hd), pg_map(i)) for i in range(n_pg)]
        + [pl.BlockSpec((1, PAGE_SIZE, hd), pg_map(i)) for i in range(n_pg)]
        + [pl.BlockSpec((1, 8, MOBA_BLOCK), lambda b, h, ph, bl: (h, 0, 0))],
        out_specs=vec)
    return pl.pallas_call(
        functools.partial(_decode_body, last_block=last_block), grid_spec=grid_spec,
        out_shape=jax.ShapeDtypeStruct((n_batch, N_HEADS, 1, hd), F32),
        compiler_params=_cparams(("arbitrary", "arbitrary"), 32), name="decode")(
            phys, blks, q4, kn4, vn4, *([kpool] * n_pg), *([vpool] * n_pg), tabs)


def _sconv_body(p_ref, cs_ref, cw_ref, cb_ref, dtb_ref, alog_ref, x_ref, b_ref, c_ref, dt_ref, da_ref, nc_ref):
    xbc = p_ref[:, COL_X:COL_X + CONV_DIM]
    cs = cs_ref[...]
    conv = cb_ref[...] + xbc * cw_ref[CONV_WIDTH - 1:CONV_WIDTH, :]
    for i in range(CONV_WIDTH - 1):
        conv = conv + cs[:, i * CONV_DIM:(i + 1) * CONV_DIM] * cw_ref[i:i + 1, :]
    nc_ref[:, 0:(CONV_WIDTH - 2) * CONV_DIM] = cs[:, CONV_DIM:]
    nc_ref[:, (CONV_WIDTH - 2) * CONV_DIM:] = xbc
    act = _silu(conv)
    gn = N_SSD_GROUPS * SSD_STATE
    x_ref[...] = act[:, :SSD_WIDTH]
    b_ref[...] = act[:, SSD_WIDTH:SSD_WIDTH + gn]
    c_ref[...] = act[:, SSD_WIDTH + gn:]
    dtv = p_ref[:, COL_DT:COL_DT + LANES] + dtb_ref[...]
    dt = jnp.maximum(dtv, 0.0) + jnp.log1p(jnp.exp(-jnp.abs(dtv)))
    dt_ref[...] = dt
    da_ref[...] = jnp.exp(dt * (-jnp.exp(alog_ref[...])))


def _sconv(proj_s, conv_state2, conv_w, conv_b, dt_bias_p, a_log_p):
    n = proj_s.shape[0]
    gn = N_SSD_GROUPS * SSD_STATE
    shapes = [(n, SSD_WIDTH), (n, gn), (n, gn), (n, LANES), (n, LANES), (n, (CONV_WIDTH - 1) * CONV_DIM)]
    return pl.pallas_call(
        _sconv_body,
        out_shape=[jax.ShapeDtypeStruct(s, F32) for s in shapes],
        compiler_params=pltpu.CompilerParams(vmem_limit_bytes=32 * 1024 * 1024), name="sconv")(
            proj_s, conv_state2, conv_w, conv_b, dt_bias_p, a_log_p)


def _sstate_body(st_ref, x_ref, dt_ref, da_ref, b_ref, c_ref, ns_ref, y_ref):
    hpg = N_SSD_HEADS // N_SSD_GROUPS
    for g in range(N_SSD_GROUPS):
        hs = slice(g * hpg, (g + 1) * hpg)
        st = st_ref[0, hs]
        xdt = x_ref[0, hs] * dt_ref[0, hs]
        ns = st * da_ref[0, hs] + xdt * b_ref[0, g]
        ns_ref[0, hs] = ns
        y_ref[0, hs] = jnp.sum(ns * c_ref[0, g], axis=-1, keepdims=True)


def _sstate(state, x4, dt4, da4, b4, c4):
    n = state.shape[0]
    h, p, s = N_SSD_HEADS, SSD_HEAD_DIM, SSD_STATE
    g = N_SSD_GROUPS
    m4 = lambda b: (b, 0, 0, 0)
    return pl.pallas_call(
        _sstate_body, grid=(n,),
        in_specs=[pl.BlockSpec((1, h, p, s), m4), pl.BlockSpec((1, h, p, 1), m4),
                  pl.BlockSpec((1, h, 1, 1), m4), pl.BlockSpec((1, h, 1, 1), m4),
                  pl.BlockSpec((1, g, 1, s), m4), pl.BlockSpec((1, g, 1, s), m4)],
        out_specs=[pl.BlockSpec((1, h, p, s), m4), pl.BlockSpec((1, h, p, 1), m4)],
        out_shape=[jax.ShapeDtypeStruct((n, h, p, s), F32), jax.ShapeDtypeStruct((n, h, p, 1), F32)],
        compiler_params=_cparams(("arbitrary",), 32), name="sstate")(state, x4, dt4, da4, b4, c4)


def _sgate_body(y_ref, x_ref, p_ref, dsk_ref, nrm_ref, o_ref):
    y = y_ref[...] + dsk_ref[...] * x_ref[...]
    y = y * _silu(p_ref[:, COL_Z:COL_Z + SSD_WIDTH])
    r = lax.rsqrt(jnp.mean(y * y, axis=-1, keepdims=True) + RMS_EPS)
    o_ref[...] = ((y * r) * nrm_ref[...]).astype(o_ref.dtype)


def _sgate(y, xs, proj_s, dskip_w, ssd_norm):
    return pl.pallas_call(
        _sgate_body, out_shape=jax.ShapeDtypeStruct(y.shape, BF16),
        compiler_params=pltpu.CompilerParams(vmem_limit_bytes=32 * 1024 * 1024), name="sgate")(
            y, xs, proj_s, dskip_w, ssd_norm)


def kernel(x_prompt, x_sample, cache_k, cache_v, state_ssm, state_conv, page_table, c_prompt, c_sample,
           w_ada, b_ada, norm_mix, norm_ffn, w_in, conv_w, conv_b, dt_bias, a_log, d_skip, ssd_norm,
           w_out, rel_bias, w_router, router_bias, w_gate_e, w_up_e, w_down_e, w_gate_s, w_up_s, w_down_s,
           norm_final):
    depth = w_ada.shape[0]
    assert depth == 1
    bp, sp, d = x_prompt.shape
    bs, sn, _ = x_sample.shape
    assert bp == 1 and sn == 1 and d == D_MODEL
    n_pages = page_table.shape[1]
    past_len = n_pages * PAGE_SIZE
    nb_prompt = sp // MOBA_BLOCK
    nb_past = past_len // MOBA_BLOCK
    l = 0

    w_in_b = jnp.pad(w_in[l].astype(BF16), ((0, 0), (0, IN_COLS_PAD - IN_COLS)))
    w_out_b = w_out[l].astype(BF16)
    wgs_b, wus_b, wds_b = w_gate_s[l].astype(BF16), w_up_s[l].astype(BF16), w_down_s[l].astype(BF16)
    wr2 = jnp.concatenate([w_router[l], w_router[l]], axis=1)
    wr_hi = wr2.astype(BF16)
    wr_lo = (wr2 - wr_hi.astype(F32)).astype(BF16)
    rb2 = jnp.concatenate([router_bias[l], router_bias[l]])[None, :].astype(F32)
    g_mix = norm_mix[l][None, :]
    g_ffn = norm_ffn[l][None, :]
    g_fin = norm_final[None, :]
    pad_h = LANES - N_SSD_HEADS
    dt_bias_p = jnp.pad(dt_bias[l].astype(F32), (0, pad_h))[None, :]
    a_log_p = jnp.pad(a_log[l].astype(F32), (0, pad_h))[None, :]
    dskip_w = jnp.repeat(d_skip[l].astype(F32), SSD_HEAD_DIM)[None, :]
    nrm_ssd = ssd_norm[l][None, :]
    cw = conv_w[l]
    cbias = conv_b[l][None, :]
    expand = _head_expand()
    expand_t = expand.T
    own_bias, prev_bias = _prompt_bias_tables(rel_bias)
    rel_t = rel_bias.T

    c_all = jnp.concatenate([c_prompt, jnp.zeros((8 - bp, d), F32), c_sample], axis=0)
    mod = _adaln(c_all, w_ada[l], b_ada[l][None, :])
    mod_p = mod[0:1]
    mod_s = mod[8:8 + bs]

    xp = x_prompt.reshape(sp, d)
    proj = _inproj(xp, g_mix, mod_p, w_in_b, 1024)
    k_p, v_p, kb, vb, kmean = _kvpost(proj)
    kmean_pad = jnp.pad(kmean.reshape(nb_prompt, ATT_WIDTH), ((0, LANES - nb_prompt), (0, 0)))
    att = _moba_prompt(proj, kb, vb, kmean_pad, own_bias, prev_bias)
    ssd, ssm_p, conv_tail = _ssd_prompt(proj, cw, cbias, dt_bias_p, a_log_p, dskip_w, nrm_ssd, expand, expand_t)
    xp1 = _outproj(att, ssd, w_out_b, xp, mod_p, 1024)

    xs = x_sample.reshape(bs, d)
    proj_s = _inproj(xs, g_mix, mod_s, w_in_b, bs)
    q_s = proj_s[:, COL_Q:COL_Q + ATT_WIDTH]
    k_s = proj_s[:, COL_K:COL_K + ATT_WIDTH]
    v_s = proj_s[:, COL_V:COL_V + ATT_WIDTH]
    kpool = cache_k[l].reshape(cache_k.shape[1], PAGE_SIZE, ATT_WIDTH)
    vpool = cache_v[l].reshape(cache_v.shape[1], PAGE_SIZE, ATT_WIDTH)
    ksum = _pagesum(page_table.reshape(-1), kpool, bs, n_pages)
    sel = _gate3(q_s.reshape(bs, 1, ATT_WIDTH), ksum)
    blks = jnp.transpose(sel[:, :MOBA_TOPK, :], (0, 2, 1))
    ppb = MOBA_BLOCK // PAGE_SIZE
    lpage = blks[..., None] * ppb + jnp.arange(ppb, dtype=I32)
    phys = jnp.take_along_axis(page_table, lpage.reshape(bs, -1), axis=1)
    dist_last = past_len - ((nb_past - 1) * MOBA_BLOCK + jnp.arange(MOBA_BLOCK))
    tab_last = rel_t[:, _rel_bucket(dist_last)]
    tab_far = jnp.broadcast_to(rel_t[:, NUM_BUCKETS - 1:NUM_BUCKETS], tab_last.shape)
    tab_self = jnp.broadcast_to(rel_t[:, 0:1], tab_last.shape)
    tabs = jnp.stack([tab_last, tab_far, tab_self] + [jnp.zeros_like(tab_last)] * 5, axis=1).astype(F32)
    shape4 = (bs, N_HEADS, 1, HEAD_DIM)
    att_s = _decode(phys.reshape(-1).astype(I32), blks.reshape(-1).astype(I32), q_s.reshape(shape4),
                    k_s.reshape(shape4), v_s.reshape(shape4), kpool, vpool, tabs, nb_past - 1)
    att_s = att_s.reshape(bs, ATT_WIDTH).astype(BF16)

    xs_act, b_s, c_s, dt_s, da_s, conv_s = _sconv(
        proj_s, state_conv[l].reshape(bs, (CONV_WIDTH - 1) * CONV_DIM), cw, cbias, dt_bias_p, a_log_p)
    ssm_s, y_s = _sstate(
        state_ssm[l], xs_act.reshape(bs, N_SSD_HEADS, SSD_HEAD_DIM, 1),
        dt_s[:, :N_SSD_HEADS].reshape(bs, N_SSD_HEADS, 1, 1), da_s[:, :N_SSD_HEADS].reshape(bs, N_SSD_HEADS, 1, 1),
        b_s.reshape(bs, N_SSD_GROUPS, 1, SSD_STATE), c_s.reshape(bs, N_SSD_GROUPS, 1, SSD_STATE))
    ssd_s = _sgate(y_s.reshape(bs, SSD_WIDTH), xs_act, proj_s, dskip_w, nrm_ssd)
    xs1 = _outproj(att_s, ssd_s, w_out_b, xs, mod_s, bs)

    n_tok = sp + bs
    spare_tok = 2 * MOE_ROWS // TOP_K
    tail = 128
    assert tail >= bs + spare_tok and sp % tail == 0
    hp, idx_p, wt_p = _ffnpre(xp1, g_ffn, mod_p, wr_hi, wr_lo, rb2, 256, hp_rows=sp + 256)
    hp, idx_s, wt_s = _ffnpre(xs1, g_ffn, mod_s, wr_hi, wr_lo, rb2, bs, hp_prev=hp, hp_rows=(sp, tail))
    idx = jnp.concatenate([idx_p[:, :TOP_K], idx_s[:, :TOP_K]], axis=0)
    wts = jnp.concatenate([wt_p[:, :TOP_K], wt_s[:, :TOP_K]], axis=0)
    blk_e, n_live, row_dst, row_w = _dispatch(idx, wts, n_tok)
    y8_rows = (n_tok + spare_tok) * TOP_K
    y8 = _moe(blk_e, n_live, row_dst, row_w, hp, w_gate_e[l], w_up_e[l], w_down_e[l], n_tok * TOP_K, y8_rows)
    y8v = y8.reshape(n_tok + spare_tok, TOP_K * (d // 2))
    y_p = _combine(xp1, y8v, hp, wgs_b, wus_b, wds_b, mod_p, g_fin, 256, 0)
    y_s_out = _combine(xs1, y8v, hp, wgs_b, wus_b, wds_b, mod_s, g_fin, bs, sp)

    kv_shape = (1, bp, sp, N_HEADS, HEAD_DIM)
    kvs_shape = (1, bs, sn, N_HEADS, HEAD_DIM)
    return (y_p.reshape(bp, sp, d), y_s_out.reshape(bs, sn, d),
            k_p.reshape(kv_shape), v_p.reshape(kv_shape),
            k_s.reshape(kvs_shape), v_s.reshape(kvs_shape),
            ssm_p.reshape(1, bp, N_SSD_HEADS, SSD_HEAD_DIM, SSD_STATE),
            ssm_s.reshape(1, bs, N_SSD_HEADS, SSD_HEAD_DIM, SSD_STATE),
            conv_tail[8 - (CONV_WIDTH - 1):].reshape(1, bp, CONV_WIDTH - 1, CONV_DIM),
            conv_s.reshape(1, bs, CONV_WIDTH - 1, CONV_DIM))
```

```python
import functools
import math

import jax
import jax.numpy as jnp
from jax import lax
from jax.experimental import pallas as pl
from jax.experimental.pallas import tpu as pltpu

F32 = jnp.float32
BF16 = jnp.bfloat16
I32 = jnp.int32
U32 = jnp.uint32

D_MODEL = 2048
ATT_WIDTH = 1024
HEAD_DIM = 128
N_HEADS = 8
MOBA_BLOCK = 256
MOBA_TOPK = 3
Q_BLOCK = 128
NUM_BUCKETS = 32
MAX_DISTANCE = 128
SSD_WIDTH = 1024
SSD_HEAD_DIM = 64
N_SSD_HEADS = 16
N_SSD_GROUPS = 2
SSD_STATE = 128
CONV_WIDTH = 4
SSD_CHUNK = 256
CONV_DIM = SSD_WIDTH + 2 * N_SSD_GROUPS * SSD_STATE
N_EXPERTS = 64
TOP_K = 8
N_EXPERT_GROUPS = 8
TOPK_GROUPS = 4
D_EXPERT = 512
ROUTED_SCALE = 2.5
RMS_EPS = 1e-6
PAGE_SIZE = 128

LANES = 128
TILE_ROWS = 8
LOG2E = 1.4426950408889634
MOBA_GROUP = 8
RANK_BITS = 20
ROUTE_CHUNK = 256
COL_Q, COL_K, COL_V, COL_Z, COL_X = 0, 1024, 2048, 3072, 4096
COL_B, COL_C, COL_DT = 5120, 5376, 5632
IN_COLS = 5648
IN_COLS_PAD = 6144
PROJ_TN = 512
MOE_ROWS = 256
MASK_BIG = 2.0 ** 40
NT_DIMS = (((1,), (1,)), ((), ()))


def _cparams(sem, vmem_mb):
    return pltpu.CompilerParams(dimension_semantics=sem, vmem_limit_bytes=vmem_mb * 1024 * 1024)


def _silu(x):
    return x * jax.nn.sigmoid(x)


def _norm_mod(x, g, shift, scale):
    r = lax.rsqrt(jnp.mean(x * x, axis=-1, keepdims=True) + RMS_EPS)
    return (x * r) * g * (1.0 + scale) + shift


def _split2(x):
    hi = x.astype(BF16)
    lo = (x - hi.astype(F32)).astype(BF16)
    return hi, lo


def _split3(x):
    h1 = x.astype(BF16)
    r1 = x - h1.astype(F32)
    h2 = r1.astype(BF16)
    h3 = (r1 - h2.astype(F32)).astype(BF16)
    return h1, h2, h3


def _pack_pair(lo, hi):
    lo_b = lax.bitcast_convert_type(lo.astype(BF16).astype(F32), U32) >> 16
    hi_b = lax.bitcast_convert_type(hi.astype(BF16).astype(F32), U32) & jnp.uint32(0xFFFF0000)
    return lo_b | hi_b


def _unpack_pair(u):
    lo = lax.bitcast_convert_type(u << 16, F32)
    hi = lax.bitcast_convert_type(u & jnp.uint32(0xFFFF0000), F32)
    return lo, hi


def _adaln_body(c_ref, w_ref, b_ref, o_ref):
    s = _silu(c_ref[...])
    o_ref[...] = jnp.dot(s.astype(BF16), w_ref[...].astype(BF16), preferred_element_type=F32) + b_ref[...]


def _adaln(c_all, w, b):
    r, d = c_all.shape
    n = w.shape[1]
    tn = 1024
    return pl.pallas_call(
        _adaln_body, grid=(n // tn,),
        in_specs=[pl.BlockSpec((r, d), lambda j: (0, 0)),
                  pl.BlockSpec((d, tn), lambda j: (0, j)),
                  pl.BlockSpec((1, tn), lambda j: (0, j))],
        out_specs=pl.BlockSpec((r, tn), lambda j: (0, j)),
        out_shape=jax.ShapeDtypeStruct((r, n), F32),
        compiler_params=_cparams(("arbitrary",), 40), name="adaln")(c_all, w, b)


def _inproj_body(x_ref, g_ref, sh_ref, sc_ref, w_ref, o_ref, h_scr):
    @pl.when(pl.program_id(1) == 0)
    def _():
        h_scr[...] = _norm_mod(x_ref[...], g_ref[...], sh_ref[...], sc_ref[...]).astype(BF16)

    o_ref[...] = jnp.dot(h_scr[...], w_ref[...], preferred_element_type=F32)


def _inproj(x, g, mod, w_pad, tm):
    t, d = x.shape
    r = mod.shape[0]
    n = w_pad.shape[1]
    return pl.pallas_call(
        _inproj_body, grid=(t // tm, n // PROJ_TN),
        in_specs=[pl.BlockSpec((tm, d), lambda i, j: (i, 0)),
                  pl.BlockSpec((1, d), lambda i, j: (0, 0)),
                  pl.BlockSpec((r, d), lambda i, j: (0, 0)),
                  pl.BlockSpec((r, d), lambda i, j: (0, 1)),
                  pl.BlockSpec((d, PROJ_TN), lambda i, j: (0, j))],
        out_specs=pl.BlockSpec((tm, PROJ_TN), lambda i, j: (i, j)),
        out_shape=jax.ShapeDtypeStruct((t, n), F32),
        scratch_shapes=[pltpu.VMEM((tm, d), BF16)],
        compiler_params=_cparams(("arbitrary", "arbitrary"), 48), name="inproj")(x, g, mod, mod, w_pad)


def _kvpost_body(k_ref, v_ref, ko, vo, kao, vto, kmo):
    k = k_ref[...]
    v = v_ref[...]
    ko[...] = k
    vo[...] = v
    rows = k.shape[0]
    row = lax.broadcasted_iota(I32, (rows, LANES), 0)
    lane = lax.broadcasted_iota(I32, (rows, LANES), 1)
    blk = pl.program_id(0) * (rows // MOBA_BLOCK) + row // MOBA_BLOCK
    onehot = jnp.where(lane == blk, 1.0, 0.0).astype(BF16)
    kb = k.astype(BF16)
    pieces = []
    for h in range(N_HEADS):
        pieces += [kb[:, h * HEAD_DIM:(h + 1) * HEAD_DIM], onehot]
    kao[...] = jnp.concatenate(pieces, axis=1)
    vto[...] = v.T.astype(BF16)
    for i in range(rows // MOBA_BLOCK):
        kmo[i] = jnp.sum(k[i * MOBA_BLOCK:(i + 1) * MOBA_BLOCK], axis=0, keepdims=True) * (1.0 / MOBA_BLOCK)


def _kvpost(proj):
    t = proj.shape[0]
    r = 512
    w = ATT_WIDTH
    nb = t // MOBA_BLOCK
    return pl.pallas_call(
        _kvpost_body, grid=(t // r,),
        in_specs=[pl.BlockSpec((r, w), lambda i: (i, COL_K // w)),
                  pl.BlockSpec((r, w), lambda i: (i, COL_V // w))],
        out_specs=[pl.BlockSpec((r, w), lambda i: (i, 0)),
                   pl.BlockSpec((r, w), lambda i: (i, 0)),
                   pl.BlockSpec((r, 2 * w), lambda i: (i, 0)),
                   pl.BlockSpec((w, r), lambda i: (0, i)),
                   pl.BlockSpec((r // MOBA_BLOCK, 1, w), lambda i: (i, 0, 0))],
        out_shape=[jax.ShapeDtypeStruct((t, w), F32), jax.ShapeDtypeStruct((t, w), F32),
                   jax.ShapeDtypeStruct((t, 2 * w), BF16), jax.ShapeDtypeStruct((w, t), BF16),
                   jax.ShapeDtypeStruct((nb, 1, w), F32)],
        compiler_params=_cparams(("arbitrary",), 40), name="kvpost")(proj, proj)


def _rel_bucket(dist):
    d = jnp.maximum(dist, 0)
    max_exact = NUM_BUCKETS // 2
    ratio = jnp.log(jnp.maximum(d, 1).astype(F32) / max_exact) / math.log(MAX_DISTANCE / max_exact)
    large = jnp.minimum(max_exact + (ratio * (NUM_BUCKETS - max_exact)).astype(I32), NUM_BUCKETS - 1)
    return jnp.where(d < max_exact, d, large)


def _prompt_bias_tables(rel_bias):
    rel_t = rel_bias.T
    far = rel_t[:, NUM_BUCKETS - 1][:, None, None]
    j = jnp.arange(MOBA_BLOCK)[:, None]
    i = jnp.arange(MOBA_BLOCK)[None, :]
    d_own = i - j
    own = jnp.where(d_own >= 0, (rel_t[:, _rel_bucket(d_own)] - far) * LOG2E, -MASK_BIG)
    prev = (rel_t[:, _rel_bucket(d_own + MOBA_BLOCK)] - far) * LOG2E
    return own.astype(F32), prev.astype(F32)


def _moba_body(q_ref, ka_ref, vt_ref, km_ref, ob_ref, pb_ref, o_ref):
    own = pl.program_id(1)
    nq = MOBA_BLOCK
    hd = HEAD_DIM
    dotf = functools.partial(jnp.dot, preferred_element_type=F32)
    q_t = q_ref[...].T
    q_hi, q_lo = _split2(q_t)
    km_hi, km_lo = _split2(km_ref[...])
    gate = dotf(km_hi, q_hi) + dotf(km_hi, q_lo) + dotf(km_lo, q_hi)
    rowi = lax.broadcasted_iota(I32, gate.shape, 0)
    row_f = rowi.astype(F32)
    g = jnp.where(rowi < own, gate, -jnp.inf)
    sel = jnp.zeros(gate.shape, F32)
    for _ in range(MOBA_TOPK):
        m = jnp.max(g, axis=0, keepdims=True)
        idx = jnp.min(jnp.where(g == m, row_f, 1e9), axis=0, keepdims=True)
        pick = row_f == idx
        sel = jnp.where(pick, 1.0, sel)
        g = jnp.where(pick, -jnp.inf, g)
    sel = jnp.where(rowi < own, sel, 0.0)
    mq_t = jnp.where(jnp.where(rowi < own - 1, sel, 0.0) > 0.0, 0.0, -MASK_BIG).astype(BF16)
    qs_t = (q_t * (hd ** -0.5 * LOG2E)).astype(BF16)
    q_aug_t = jnp.concatenate([qs_t, mq_t], axis=0)

    prev = jnp.maximum(own - 1, 0)
    st_own = pl.multiple_of(own * MOBA_BLOCK, MOBA_BLOCK)
    st_prev = pl.multiple_of(prev * MOBA_BLOCK, MOBA_BLOCK)
    s_own = dotf(ka_ref[pl.ds(st_own, MOBA_BLOCK), 0:hd], qs_t) + ob_ref[0]
    sel_prev = jnp.max(jnp.where(rowi == own - 1, sel, 0.0), axis=0, keepdims=True)
    s_prev = dotf(ka_ref[pl.ds(st_prev, MOBA_BLOCK), 0:hd], qs_t) + pb_ref[0] \
        + jnp.where(sel_prev > 0.0, 0.0, -MASK_BIG)
    m0 = jnp.maximum(jnp.max(s_own, axis=0, keepdims=True), jnp.max(s_prev, axis=0, keepdims=True))
    p_own = jnp.exp2(s_own - m0)
    p_prev = jnp.exp2(s_prev - m0)
    l0 = jnp.sum(p_own, axis=0, keepdims=True) + jnp.sum(p_prev, axis=0, keepdims=True)
    acc0 = dotf(vt_ref[:, pl.ds(st_own, MOBA_BLOCK)], p_own.astype(BF16)) \
        + dotf(vt_ref[:, pl.ds(st_prev, MOBA_BLOCK)], p_prev.astype(BF16))

    span = MOBA_GROUP * MOBA_BLOCK

    def group(it, carry):
        m, l, acc = carry
        st = pl.multiple_of(it * span, span)
        s = dotf(ka_ref[pl.ds(st, span), :], q_aug_t)
        m_new = jnp.maximum(m, jnp.max(s, axis=0, keepdims=True))
        p = jnp.exp2(s - m_new)
        alpha = jnp.exp2(m - m_new)
        l = alpha * l + jnp.sum(p, axis=0, keepdims=True)
        acc = alpha * acc + dotf(vt_ref[:, pl.ds(st, span)], p.astype(BF16))
        return m_new, l, acc

    n_groups = (jnp.maximum(own - 1, 0) + MOBA_GROUP - 1) // MOBA_GROUP
    _, l, acc = lax.fori_loop(0, n_groups, group, (m0, l0, acc0))
    o_ref[...] = (acc * (1.0 / l)).T.astype(o_ref.dtype)


def _moba_prompt(proj, kaug, v_t, kmean_pad, own_bias, prev_bias):
    t = proj.shape[0]
    hd = HEAD_DIM
    nb = t // MOBA_BLOCK
    assert nb % MOBA_GROUP == 0 and nb <= LANES
    return pl.pallas_call(
        _moba_body, grid=(N_HEADS, nb),
        in_specs=[pl.BlockSpec((MOBA_BLOCK, hd), lambda h, c: (c, h)),
                  pl.BlockSpec((t, 2 * hd), lambda h, c: (0, h)),
                  pl.BlockSpec((hd, t), lambda h, c: (h, 0)),
                  pl.BlockSpec((LANES, hd), lambda h, c: (0, h)),
                  pl.BlockSpec((1, MOBA_BLOCK, MOBA_BLOCK), lambda h, c: (h, 0, 0)),
                  pl.BlockSpec((1, MOBA_BLOCK, MOBA_BLOCK), lambda h, c: (h, 0, 0))],
        out_specs=pl.BlockSpec((MOBA_BLOCK, hd), lambda h, c: (c, h)),
        out_shape=jax.ShapeDtypeStruct((t, ATT_WIDTH), BF16),
        compiler_params=_cparams(("arbitrary", "arbitrary"), 56), name="moba_prompt")(
            proj, kaug, v_t, kmean_pad, own_bias, prev_bias)


def _ssd_body(z_ref, x_ref, b_ref, c_ref, dt_ref, cw_ref, cb_ref, dtb_ref, alog_ref, dsk_ref, nrm_ref,
              exp_ref, expt_ref, y_ref, st_ref, tail_ref, xpad, state):
    i = pl.program_id(0)
    L = SSD_CHUNK
    W = SSD_WIDTH
    GN = N_SSD_GROUPS * SSD_STATE

    @pl.when(i == 0)
    def _():
        xpad[0:8, :] = jnp.zeros((8, CONV_DIM), F32)
        state[...] = jnp.zeros(state.shape, F32)

    xpad[8:8 + L, 0:W] = x_ref[...]
    xpad[8:8 + L, W:W + GN] = b_ref[...]
    xpad[8:8 + L, W + GN:W + 2 * GN] = c_ref[...]
    conv = cb_ref[...] + xpad[5:5 + L, :] * cw_ref[0:1, :]
    for r in range(1, CONV_WIDTH):
        conv = conv + xpad[5 + r:5 + r + L, :] * cw_ref[r:r + 1, :]
    tail = xpad[L:L + 8, :]
    tail_ref[...] = tail
    xpad[0:8, :] = tail
    act = _silu(conv)
    xs = act[:, 0:W]
    bm = act[:, W:W + GN].astype(BF16)
    cm = act[:, W + GN:W + 2 * GN].astype(BF16)

    dtv = dt_ref[...] + dtb_ref[...]
    dt = jnp.maximum(dtv, 0.0) + jnp.log1p(jnp.exp(-jnp.abs(dtv)))
    a = -jnp.exp(alog_ref[...])
    adt = dt * a
    row = lax.broadcasted_iota(I32, (L, L), 0)
    col = lax.broadcasted_iota(I32, (L, L), 1)
    causal = row >= col
    tril = jnp.where(causal, 1.0, 0.0).astype(BF16)
    dotf = functools.partial(jnp.dot, preferred_element_type=F32)
    acum = sum(dotf(tril, part) for part in _split3(adt))
    acum_t = acum.T
    expand = exp_ref[...]

    def widen(v):
        return sum(dotf(part, expand) for part in _split3(v))

    xd = xs * widen(dt)
    xd_b = xd.astype(BF16)
    lane = lax.broadcasted_iota(I32, (L, LANES), 1)
    low_half = lane < SSD_HEAD_DIM
    dg = functools.partial(lax.dot_general, dimension_numbers=NT_DIMS, preferred_element_type=F32)
    hpg = N_SSD_HEADS // N_SSD_GROUPS
    y_cols = []
    cbs = [dg(cm[:, g * SSD_STATE:(g + 1) * SSD_STATE], bm[:, g * SSD_STATE:(g + 1) * SSD_STATE])
           for g in range(N_SSD_GROUPS)]
    for pair in range(N_SSD_HEADS // 2):
        cb = cbs[(2 * pair) // hpg]
        xj = xd_b[:, pair * LANES:(pair + 1) * LANES]
        acc = None
        for sub in range(2):
            h = 2 * pair + sub
            diff = acum[:, h:h + 1] - acum_t[h:h + 1, :]
            dec = jnp.exp(jnp.where(causal, diff, -jnp.inf))
            mh = (cb * dec).astype(BF16)
            xh = jnp.where(low_half if sub == 0 else jnp.logical_not(low_half), xj, jnp.zeros_like(xj))
            part = dotf(mh, xh)
            acc = part if acc is None else acc + part
        y_cols.append(acc)
    y_diag = jnp.concatenate(y_cols, axis=1)

    s_prev = state[...]
    s_b = s_prev.astype(BF16)
    half_w = W // N_SSD_GROUPS
    y_off = jnp.concatenate(
        [dg(cm[:, g * SSD_STATE:(g + 1) * SSD_STATE], s_b[g * half_w:(g + 1) * half_w, :])
         for g in range(N_SSD_GROUPS)], axis=1) * widen(jnp.exp(acum))

    dte = jnp.exp(acum[L - 1:L, :] - acum)
    xdd = xd * widen(dte)
    upd = jnp.concatenate(
        [dotf(xdd[:, g * half_w:(g + 1) * half_w].T.astype(BF16), bm[:, g * SSD_STATE:(g + 1) * SSD_STATE])
         for g in range(N_SSD_GROUPS)], axis=0)
    last = jnp.broadcast_to(acum_t[:, L - 1:L], (LANES, SSD_STATE))
    dec_rows = jnp.exp(sum(dotf(expt_ref[...], part) for part in _split3(last)))
    s_new = s_prev * dec_rows + upd
    state[...] = s_new
    st_ref[...] = s_new

    y = y_diag + y_off + dsk_ref[...] * xs
    y = y * _silu(z_ref[...])
    r = lax.rsqrt(jnp.mean(y * y, axis=-1, keepdims=True) + RMS_EPS)
    y_ref[...] = ((y * r) * nrm_ref[...]).astype(y_ref.dtype)


def _head_expand():
    h = jnp.arange(LANES)[:, None]
    ch = jnp.arange(SSD_WIDTH)[None, :]
    return jnp.where(ch // SSD_HEAD_DIM == h, 1.0, 0.0).astype(BF16)


def _ssd_prompt(proj, conv_w, conv_b, dt_bias_p, a_log_p, dskip_w, ssd_norm, expand, expand_t):
    t = proj.shape[0]
    L = SSD_CHUNK
    W = SSD_WIDTH
    GN = N_SSD_GROUPS * SSD_STATE
    const = lambda shape: pl.BlockSpec(shape, lambda i: (0, 0))
    return pl.pallas_call(
        _ssd_body, grid=(t // L,),
        in_specs=[pl.BlockSpec((L, W), lambda i: (i, COL_Z // W)),
                  pl.BlockSpec((L, W), lambda i: (i, COL_X // W)),
                  pl.BlockSpec((L, GN), lambda i: (i, COL_B // GN)),
                  pl.BlockSpec((L, GN), lambda i: (i, COL_C // GN)),
                  pl.BlockSpec((L, LANES), lambda i: (i, COL_DT // LANES)),
                  const((CONV_WIDTH, CONV_DIM)), const((1, CONV_DIM)), const((1, LANES)), const((1, LANES)),
                  const((1, W)), const((1, W)), const((LANES, W)), const((W, LANES))],
        out_specs=[pl.BlockSpec((L, W), lambda i: (i, 0)),
                   pl.BlockSpec((W, SSD_STATE), lambda i: (0, 0)),
                   pl.BlockSpec((8, CONV_DIM), lambda i: (0, 0))],
        out_shape=[jax.ShapeDtypeStruct((t, W), BF16),
                   jax.ShapeDtypeStruct((W, SSD_STATE), F32),
                   jax.ShapeDtypeStruct((8, CONV_DIM), F32)],
        scratch_shapes=[pltpu.VMEM((L + 8, CONV_DIM), F32), pltpu.VMEM((W, SSD_STATE), F32)],
        compiler_params=_cparams(("arbitrary",), 40), name="ssd_prompt")(
            proj, proj, proj, proj, proj, conv_w, conv_b, dt_bias_p, a_log_p, dskip_w, ssd_norm, expand, expand_t)


def _outproj_body(a_ref, s_ref, w_ref, x_ref, g_ref, o_ref):
    w = ATT_WIDTH
    acc = jnp.dot(a_ref[...], w_ref[0:w, :], preferred_element_type=F32)
    acc = acc + jnp.dot(s_ref[...], w_ref[w:, :], preferred_element_type=F32)
    o_ref[...] = x_ref[...] + g_ref[...] * acc


def _outproj(att, ssd, w_out_b, x, mod, tm):
    t, d = x.shape
    r = mod.shape[0]
    tn = PROJ_TN
    return pl.pallas_call(
        _outproj_body, grid=(t // tm, d // tn),
        in_specs=[pl.BlockSpec((tm, ATT_WIDTH), lambda i, j: (i, 0)),
                  pl.BlockSpec((tm, SSD_WIDTH), lambda i, j: (i, 0)),
                  pl.BlockSpec((ATT_WIDTH + SSD_WIDTH, tn), lambda i, j: (0, j)),
                  pl.BlockSpec((tm, tn), lambda i, j: (i, j)),
                  pl.BlockSpec((r, tn), lambda i, j: (0, 2 * (d // tn) + j))],
        out_specs=pl.BlockSpec((tm, tn), lambda i, j: (i, j)),
        out_shape=jax.ShapeDtypeStruct((t, d), F32),
        compiler_params=_cparams(("arbitrary", "arbitrary"), 40), name="outproj")(att, ssd, w_out_b, x, mod)


def _group_allreduce(v, lane, op):
    for s in (1, 2, 4):
        up = pltpu.roll(v, LANES - s, axis=1)
        dn = pltpu.roll(v, s, axis=1)
        v = op(v, jnp.where((lane & s) == 0, up, dn))
    return v


def _store_token_tiles(ref, packed, n_tok):
    for s in range(TILE_ROWS):
        ref[pl.ds(s, n_tok, stride=TILE_ROWS), :] = packed[:, s * LANES:(s + 1) * LANES]


def _load_token_tiles(ref, n_tok, start=0, stride=TILE_ROWS):
    return jnp.concatenate([ref[pl.ds(start + s, n_tok, stride=stride), :] for s in range(TILE_ROWS)], axis=1)


def _ffnpre_body(x_ref, g_ref, sh_ref, sc_ref, wrh_ref, wrl_ref, rb_ref, cin_ref,
                 hp_ref, idx_ref, wt_ref, rank_ref, cnt_ref, cnt_scr, *, rows, n_real):
    i = pl.program_id(0)
    real = i < n_real

    @pl.when(i == 0)
    def _():
        cnt_scr[...] = cin_ref[...]

    h = _norm_mod(x_ref[...], g_ref[...], sh_ref[...], sc_ref[...])
    half = D_MODEL // 2
    packed = _pack_pair(h[:, :half], h[:, half:])
    _store_token_tiles(hp_ref, jnp.where(real, packed, jnp.zeros_like(packed)), rows)
    if rows * TILE_ROWS != hp_ref.shape[0]:
        hp_ref[rows * TILE_ROWS:, :] = jnp.zeros((hp_ref.shape[0] - rows * TILE_ROWS, LANES), U32)
    h_hi, h_lo = _split2(h)
    dotf = functools.partial(jnp.dot, preferred_element_type=F32)
    logits = dotf(h_hi, wrh_ref[...]) + dotf(h_hi, wrl_ref[...]) + dotf(h_lo, wrh_ref[...])
    scores = jax.nn.sigmoid(logits)
    biased = scores + rb_ref[...]
    lane = lax.broadcasted_iota(I32, biased.shape, 1)
    lane_f = lane.astype(F32)
    per_group = N_EXPERTS // N_EXPERT_GROUPS
    m1 = _group_allreduce(biased, lane, jnp.maximum)
    i1 = _group_allreduce(jnp.where(biased == m1, lane_f, 1e9), lane, jnp.minimum)
    m2 = _group_allreduce(jnp.where(lane_f == i1, -jnp.inf, biased), lane, jnp.maximum)
    gs = m1 + m2
    grp = (lane // per_group) % N_EXPERT_GROUPS
    beaten = jnp.zeros(biased.shape, F32)
    for k in range(1, N_EXPERT_GROUPS):
        other = pltpu.roll(gs, per_group * k, axis=1)
        other_grp = (grp - k) % N_EXPERT_GROUPS
        wins = jnp.where(other > gs, 1.0, jnp.where(other == gs, jnp.where(other_grp < grp, 1.0, 0.0), 0.0))
        beaten = beaten + wins
    keep = jnp.where(lane < N_EXPERTS, jnp.where(beaten < TOPK_GROUPS, 1.0, 0.0), 0.0)
    masked = jnp.where(keep > 0.0, biased, -jnp.inf)
    idx_out = jnp.zeros(biased.shape, F32)
    w_out = jnp.zeros(biased.shape, F32)
    w_sum = jnp.zeros((biased.shape[0], 1), F32)
    member = jnp.zeros(biased.shape, F32)
    picks = []
    for k in range(TOP_K):
        m = jnp.max(masked, axis=1, keepdims=True)
        ik = jnp.min(jnp.where(masked == m, lane_f, 1e9), axis=1, keepdims=True)
        pick = lane_f == ik
        wk = jnp.sum(jnp.where(pick, scores, 0.0), axis=1, keepdims=True)
        masked = jnp.where(pick, -jnp.inf, masked)
        member = jnp.where(pick, 1.0, member)
        picks.append(pick)
        idx_out = jnp.where(lane == k, ik, idx_out)
        w_out = jnp.where(lane == k, wk, w_out)
        w_sum = w_sum + wk
    member = jnp.where(real, member, 0.0)
    ri = lax.broadcasted_iota(I32, (rows, rows), 0)
    ci = lax.broadcasted_iota(I32, (rows, rows), 1)
    before = jnp.where(ri > ci, 1.0, 0.0).astype(BF16)
    pos = cnt_scr[...] + dotf(before, member.astype(BF16))
    rank_out = jnp.zeros(biased.shape, F32)
    for k in range(TOP_K):
        rk = jnp.sum(jnp.where(picks[k], pos, 0.0), axis=1, keepdims=True)
        rank_out = jnp.where(lane == k, rk, rank_out)

    @pl.when(real)
    def _():
        idx_ref[...] = idx_out.astype(I32)
        wt_ref[...] = w_out / w_sum * ROUTED_SCALE
        rank_ref[...] = rank_out.astype(I32)

    total = cnt_scr[...] + jnp.sum(member, axis=0, keepdims=True)
    cnt_scr[...] = total
    cnt_ref[...] = jnp.broadcast_to(total, cnt_ref.shape)


def _ffnpre(x, g, mod, wr_hi, wr_lo, rb, cnt_in, tm, hp_prev=None, hp_tok=None):
    t, d = x.shape
    r = mod.shape[0]
    n_real = t // tm
    steps = n_real if hp_prev is not None else hp_tok // tm
    row_map = lambda i: (jnp.minimum(i, n_real - 1), 0)
    const = lambda i: (0, 0)
    in_specs = [pl.BlockSpec((tm, d), row_map),
                pl.BlockSpec((1, d), const),
                pl.BlockSpec((r, d), lambda i: (0, 3)),
                pl.BlockSpec((r, d), lambda i: (0, 4)),
                pl.BlockSpec((d, LANES), const),
                pl.BlockSpec((d, LANES), const),
                pl.BlockSpec((1, LANES), const),
                pl.BlockSpec((1, LANES), const)]
    args = [x, g, mod, mod, wr_hi, wr_lo, rb, cnt_in]
    small = [pl.BlockSpec((tm, LANES), row_map), pl.BlockSpec((tm, LANES), row_map),
             pl.BlockSpec((tm, LANES), row_map), pl.BlockSpec((TILE_ROWS, LANES), const)]
    small_shape = [jax.ShapeDtypeStruct((t, LANES), I32), jax.ShapeDtypeStruct((t, LANES), F32),
                   jax.ShapeDtypeStruct((t, LANES), I32), jax.ShapeDtypeStruct((TILE_ROWS, LANES), F32)]
    inner = functools.partial(_ffnpre_body, rows=tm, n_real=n_real)
    n_in = len(args)
    if hp_prev is None:
        out_specs = [pl.BlockSpec((tm * TILE_ROWS, LANES), lambda i: (i, 0))] + small
        out_shape = [jax.ShapeDtypeStruct((hp_tok * TILE_ROWS, LANES), U32)] + small_shape
        body = inner
        aliases = {}
    else:
        start, blk = hp_tok
        in_specs.append(pl.BlockSpec(memory_space=pl.ANY))
        args.append(hp_prev)
        out_specs = [pl.BlockSpec((blk * TILE_ROWS, LANES), lambda i: (start // blk, 0))] + small
        out_shape = [jax.ShapeDtypeStruct(hp_prev.shape, U32)] + small_shape
        body = lambda *refs: inner(*refs[:n_in], *refs[n_in + 1:])
        aliases = {n_in: 0}
    return pl.pallas_call(
        body, grid=(steps,), in_specs=in_specs, out_specs=out_specs, out_shape=out_shape,
        input_output_aliases=aliases, scratch_shapes=[pltpu.VMEM((1, LANES), F32)],
        compiler_params=_cparams(("arbitrary",), 40), name="ffnpre")(*args)


def _assignment_row(code, pstart_ref):
    return pstart_ref[code >> RANK_BITS] + (code & ((1 << RANK_BITS) - 1))


def _tile_rows(r):
    return pl.ds(pl.multiple_of(r * TILE_ROWS, TILE_ROWS), TILE_ROWS)


def _route_body(code_ref, pstart_ref, cnt_ref, hp_ref, xs_ref, ztile, sem, zsem, *, n_assign, n_rows):
    ztile[...] = jnp.zeros(ztile.shape, U32)

    def zero_rows(lo, hi):
        def zcopy(r):
            return pltpu.make_async_copy(ztile, xs_ref.at[_tile_rows(r)], zsem.at[0])

        def start(r, carry):
            zcopy(r).start()
            return carry

        def wait(r, carry):
            zcopy(r).wait()
            return carry
        lax.fori_loop(lo, hi, start, 0)
        lax.fori_loop(lo, hi, wait, 0)

    def padded_end(e):
        cnt = cnt_ref[e]
        return pstart_ref[e] + (((cnt + MOE_ROWS - 1) // MOE_ROWS) * MOE_ROWS)

    def per_expert(e, carry):
        zero_rows(pstart_ref[e] + cnt_ref[e], padded_end(e))
        return carry
    lax.fori_loop(0, N_EXPERTS, per_expert, 0)
    zero_rows(padded_end(N_EXPERTS - 1), n_rows)

    chunk = ROUTE_CHUNK
    n_chunks = n_assign // chunk

    def copy(a, sl):
        row = _assignment_row(code_ref[a], pstart_ref)
        return pltpu.make_async_copy(hp_ref.at[_tile_rows(a // TOP_K)], xs_ref.at[_tile_rows(row)], sem.at[sl])

    def for_chunk(c, fn):
        def step(r, carry):
            fn(c * chunk + r)
            return carry
        lax.fori_loop(0, chunk, step, 0, unroll=8)

    for_chunk(0, lambda a: copy(a, 0).start())

    def pipelined(c, carry):
        for_chunk(c, lambda a: copy(a, c % 2).start())
        for_chunk(c - 1, lambda a: copy(a, (c - 1) % 2).wait())
        return carry
    lax.fori_loop(1, n_chunks, pipelined, 0)
    for_chunk(n_chunks - 1, lambda a: copy(a, (n_chunks - 1) % 2).wait())


def _route(code, pstart, counts, hp, n_assign, n_rows):
    assert n_assign % ROUTE_CHUNK == 0
    grid_spec = pltpu.PrefetchScalarGridSpec(
        num_scalar_prefetch=3, grid=(1,),
        in_specs=[pl.BlockSpec(memory_space=pl.ANY)],
        out_specs=pl.BlockSpec(memory_space=pl.ANY),
        scratch_shapes=[pltpu.VMEM((TILE_ROWS, LANES), U32),
                        pltpu.SemaphoreType.DMA((2,)), pltpu.SemaphoreType.DMA((1,))])
    return pl.pallas_call(
        functools.partial(_route_body, n_assign=n_assign, n_rows=n_rows), grid_spec=grid_spec,
        out_shape=jax.ShapeDtypeStruct((n_rows * TILE_ROWS, LANES), U32),
        compiler_params=_cparams(("arbitrary",), 32), name="route")(code, pstart, counts, hp)


def _moe_body(blk_e_ref, nlive_ref, xs_ref, wg_ref, wu_ref, wd_ref, y_ref, wg_b, wu_b, wd_b):
    b = pl.program_id(0)
    live = b < nlive_ref[0]
    e = blk_e_ref[b]
    e_prev = blk_e_ref[jnp.maximum(b - 1, 0)]

    @pl.when(jnp.logical_and(live, jnp.logical_or(b == 0, e != e_prev)))
    def _():
        wg_b[...] = wg_ref[0].astype(BF16)
        wu_b[...] = wu_ref[0].astype(BF16)
        wd_b[...] = wd_ref[0].astype(BF16)

    @pl.when(live)
    def _():
        lo, hi = _unpack_pair(_load_token_tiles(xs_ref, MOE_ROWS))
        x = jnp.concatenate([lo, hi], axis=1).astype(BF16)
        dotf = functools.partial(jnp.dot, preferred_element_type=F32)
        act = (_silu(dotf(x, wg_b[...])) * dotf(x, wu_b[...])).astype(BF16)
        y = dotf(act, wd_b[...])
        half = D_MODEL // 2
        _store_token_tiles(y_ref, _pack_pair(y[:, :half], y[:, half:]), MOE_ROWS)

    @pl.when(jnp.logical_not(live))
    def _():
        y_ref[...] = jnp.zeros(y_ref.shape, U32)


def _moe(blk_e, n_live, xs, wg, wu, wd):
    n_blk = blk_e.shape[0]
    blk_rows = MOE_ROWS * TILE_ROWS
    grid_spec = pltpu.PrefetchScalarGridSpec(
        num_scalar_prefetch=2, grid=(n_blk,),
        in_specs=[pl.BlockSpec((blk_rows, LANES), lambda b, be, nl: (b, 0)),
                  pl.BlockSpec((1, D_MODEL, D_EXPERT), lambda b, be, nl: (be[b], 0, 0)),
                  pl.BlockSpec((1, D_MODEL, D_EXPERT), lambda b, be, nl: (be[b], 0, 0)),
                  pl.BlockSpec((1, D_EXPERT, D_MODEL), lambda b, be, nl: (be[b], 0, 0))],
        out_specs=pl.BlockSpec((blk_rows, LANES), lambda b, be, nl: (b, 0)),
        scratch_shapes=[pltpu.VMEM((D_MODEL, D_EXPERT), BF16), pltpu.VMEM((D_MODEL, D_EXPERT), BF16),
                        pltpu.VMEM((D_EXPERT, D_MODEL), BF16)])
    return pl.pallas_call(
        _moe_body, grid_spec=grid_spec,
        out_shape=jax.ShapeDtypeStruct(xs.shape, U32),
        compiler_params=_cparams(("arbitrary",), 48), name="moe")(blk_e, n_live, xs, wg, wu, wd)


def _expert_layout(cnt_total, n_assign):
    assert n_assign % MOE_ROWS == 0 and n_assign < (1 << RANK_BITS)
    counts = cnt_total[0, :N_EXPERTS].astype(I32)
    padded = (counts + MOE_ROWS - 1) // MOE_ROWS * MOE_ROWS
    pends = jnp.cumsum(padded)
    pstart = (pends - padded).astype(I32)
    n_blk = n_assign // MOE_ROWS + N_EXPERTS
    blk_start = jnp.arange(n_blk, dtype=I32) * MOE_ROWS
    blk_e = jnp.minimum(jnp.sum((pends[None, :] <= blk_start[:, None]).astype(I32), axis=1), N_EXPERTS - 1)
    n_live = (pends[-1] // MOE_ROWS).astype(I32).reshape(1)
    return blk_e.astype(I32), n_live, pstart, counts, n_blk * MOE_ROWS


def _combine_body(code_ref, pstart_ref, x_ref, y_ref, hp_ref, wt_ref, wgs_ref, wus_ref, wds_ref, g_ref, nf_ref,
                  o_ref, gbuf, sem, *, tm, tok0):
    i = pl.program_id(0)
    n = pl.num_programs(0)
    slot = i % 2
    per_tile = tm * TOP_K

    def copy(j, r, sl):
        row = _assignment_row(code_ref[(tok0 + j * tm) * TOP_K + r], pstart_ref)
        return pltpu.make_async_copy(y_ref.at[_tile_rows(row)], gbuf.at[sl, _tile_rows(r)], sem.at[sl])

    def for_assignments(fn):
        def step(r, carry):
            fn(r)
            return carry
        lax.fori_loop(0, per_tile, step, 0, unroll=8)

    @pl.when(i == 0)
    def _():
        for_assignments(lambda r: copy(0, r, 0).start())

    @pl.when(i + 1 < n)
    def _():
        for_assignments(lambda r: copy(i + 1, r, 1 - slot).start())

    for_assignments(lambda r: copy(i, r, slot).wait())
    wts = wt_ref[...]
    acc_lo = [None] * TILE_ROWS
    acc_hi = [None] * TILE_ROWS
    stride = TOP_K * TILE_ROWS
    for k in range(TOP_K):
        wk = wts[:, k:k + 1]
        for s in range(TILE_ROWS):
            l_k, h_k = _unpack_pair(gbuf[slot, pl.ds(k * TILE_ROWS + s, tm, stride=stride), :])
            acc_lo[s] = l_k * wk if acc_lo[s] is None else acc_lo[s] + l_k * wk
            acc_hi[s] = h_k * wk if acc_hi[s] is None else acc_hi[s] + h_k * wk
    routed = jnp.concatenate(acc_lo + acc_hi, axis=1)
    hl, hh = _unpack_pair(_load_token_tiles(hp_ref, tm))
    h = jnp.concatenate([hl, hh], axis=1).astype(BF16)
    dotf = functools.partial(jnp.dot, preferred_element_type=F32)
    act = (_silu(dotf(h, wgs_ref[...])) * dotf(h, wus_ref[...])).astype(BF16)
    shared = dotf(act, wds_ref[...])
    xo = x_ref[...] + g_ref[...] * (routed + shared)
    r = lax.rsqrt(jnp.mean(xo * xo, axis=-1, keepdims=True) + RMS_EPS)
    o_ref[...] = (xo * r) * nf_ref[...]


def _combine(code, pstart, x1, ys, hp, wts, wgs, wus, wds, mod, norm_final, tm, tok0):
    t, d = x1.shape
    r = mod.shape[0]
    o = tok0 // tm
    const = lambda i, cd, ps: (0, 0)
    grid_spec = pltpu.PrefetchScalarGridSpec(
        num_scalar_prefetch=2, grid=(t // tm,),
        in_specs=[pl.BlockSpec((tm, d), lambda i, cd, ps: (i, 0)),
                  pl.BlockSpec(memory_space=pl.ANY),
                  pl.BlockSpec((tm * TILE_ROWS, LANES), lambda i, cd, ps: (i + o, 0)),
                  pl.BlockSpec((tm, LANES), lambda i, cd, ps: (i, 0)),
                  pl.BlockSpec((d, D_EXPERT), const),
                  pl.BlockSpec((d, D_EXPERT), const),
                  pl.BlockSpec((D_EXPERT, d), const),
                  pl.BlockSpec((r, d), lambda i, cd, ps: (0, 5)),
                  pl.BlockSpec((1, d), const)],
        out_specs=pl.BlockSpec((tm, d), lambda i, cd, ps: (i, 0)),
        scratch_shapes=[pltpu.VMEM((2, tm * TOP_K * TILE_ROWS, LANES), U32), pltpu.SemaphoreType.DMA((2,))])
    return pl.pallas_call(
        functools.partial(_combine_body, tm=tm, tok0=tok0), grid_spec=grid_spec,
        out_shape=jax.ShapeDtypeStruct((t, d), F32),
        compiler_params=_cparams(("arbitrary",), 48), name="combine")(
            code, pstart, x1, ys, hp, wts, wgs, wus, wds, mod, norm_final)


PAGES_PER_STEP = 16


def _pagesum_body(pt_ref, *refs):
    ins = refs[:PAGES_PER_STEP]
    o_ref = refs[PAGES_PER_STEP]
    ppb = MOBA_BLOCK // PAGE_SIZE
    for j in range(PAGES_PER_STEP // ppb):
        acc = None
        for p in range(ppb):
            s = jnp.sum(ins[j * ppb + p][0], axis=0, keepdims=True)
            acc = s if acc is None else acc + s
        o_ref[0, j:j + 1, :] = acc


def _pagesum(page_table_flat, kpool, n_batch, n_pages):
    w = kpool.shape[2]
    ppb = MOBA_BLOCK // PAGE_SIZE
    steps = n_pages // PAGES_PER_STEP

    def in_map(p):
        return lambda b, j, pt: (pt[b * n_pages + j * PAGES_PER_STEP + p], 0, 0)

    grid_spec = pltpu.PrefetchScalarGridSpec(
        num_scalar_prefetch=1, grid=(n_batch, steps),
        in_specs=[pl.BlockSpec((1, PAGE_SIZE, w), in_map(p)) for p in range(PAGES_PER_STEP)],
        out_specs=pl.BlockSpec((1, PAGES_PER_STEP // ppb, w), lambda b, j, pt: (b, j, 0)))
    return pl.pallas_call(
        _pagesum_body, grid_spec=grid_spec,
        out_shape=jax.ShapeDtypeStruct((n_batch, n_pages // ppb, w), F32),
        compiler_params=_cparams(("arbitrary", "arbitrary"), 40), name="pagesum")(
            page_table_flat, *([kpool] * PAGES_PER_STEP))


def _gate3_body(q_ref, ks_ref, sel_ref):
    prod = ks_ref[0] * q_ref[0] * (1.0 / MOBA_BLOCK)
    g = jnp.concatenate([jnp.sum(prod[:, h * HEAD_DIM:(h + 1) * HEAD_DIM], axis=1, keepdims=True)
                         for h in range(N_HEADS)], axis=1)
    rowi = lax.broadcasted_iota(I32, g.shape, 0).astype(F32)
    picks = []
    for _ in range(MOBA_TOPK):
        m = jnp.max(g, axis=0, keepdims=True)
        ik = jnp.min(jnp.where(g == m, rowi, 1e9), axis=0, keepdims=True)
        g = jnp.where(rowi == ik, -jnp.inf, g)
        picks.append(ik)
    picks.append(jnp.zeros((8 - MOBA_TOPK, N_HEADS), F32))
    sel_ref[0] = jnp.concatenate(picks, axis=0).astype(I32)


def _gate3(q3, ksum):
    n_batch, nb, w = ksum.shape
    return pl.pallas_call(
        _gate3_body, grid=(n_batch,),
        in_specs=[pl.BlockSpec((1, 1, w), lambda b: (b, 0, 0)),
                  pl.BlockSpec((1, nb, w), lambda b: (b, 0, 0))],
        out_specs=pl.BlockSpec((1, 8, N_HEADS), lambda b: (b, 0, 0)),
        out_shape=jax.ShapeDtypeStruct((n_batch, 8, N_HEADS), I32),
        compiler_params=_cparams(("arbitrary",), 32), name="gate3")(q3, ksum)


def _decode_body(phys_ref, blk_ref, q_ref, kn_ref, vn_ref, *refs, last_block):
    ppb = MOBA_BLOCK // PAGE_SIZE
    n_pg = MOBA_TOPK * ppb
    k_refs = refs[:n_pg]
    v_refs = refs[n_pg:2 * n_pg]
    tab_ref = refs[2 * n_pg]
    o_ref = refs[2 * n_pg + 1]
    b = pl.program_id(0)
    h = pl.program_id(1)
    q = q_ref[0, 0]
    scale = HEAD_DIM ** -0.5
    qb = jnp.broadcast_to(q, (8, HEAD_DIM)).astype(BF16)
    dg = functools.partial(lax.dot_general, dimension_numbers=NT_DIMS, preferred_element_type=F32)
    s_self = jnp.sum(q * kn_ref[0, 0], axis=1, keepdims=True) * scale + tab_ref[0, 2:3, 0:1]
    scores = []
    vals = []
    for s in range(MOBA_TOPK):
        kblk = jnp.concatenate([k_refs[s * ppb + p][0] for p in range(ppb)], axis=0).astype(BF16)
        vals.append(jnp.concatenate([v_refs[s * ppb + p][0] for p in range(ppb)], axis=0).astype(BF16))
        n = blk_ref[(b * N_HEADS + h) * MOBA_TOPK + s]
        bias = jnp.where(n == last_block, tab_ref[0, 0:1, :], tab_ref[0, 1:2, :])
        scores.append(dg(qb, kblk)[0:1, :] * scale + bias)
    m = s_self
    for sc in scores:
        m = jnp.maximum(m, jnp.max(sc, axis=1, keepdims=True))
    p_self = jnp.exp(s_self - m)
    l = p_self
    acc = p_self * vn_ref[0, 0]
    for sc, vblk in zip(scores, vals):
        p = jnp.exp(sc - m)
        l = l + jnp.sum(p, axis=1, keepdims=True)
        pb = jnp.broadcast_to(p, (8, MOBA_BLOCK)).astype(BF16)
        acc = acc + jnp.dot(pb, vblk, preferred_element_type=F32)[0:1, :]
    o_ref[0, 0] = acc * (1.0 / l)


def _decode(phys, blks, q4, kn4, vn4, kpool, vpool, tabs, last_block):
    n_batch = q4.shape[0]
    ppb = MOBA_BLOCK // PAGE_SIZE
    n_pg = MOBA_TOPK * ppb
    hd = HEAD_DIM

    def pg_map(i):
        return lambda b, h, ph, bl: (ph[(b * N_HEADS + h) * n_pg + i], 0, h)

    vec = pl.BlockSpec((1, 1, 1, hd), lambda b, h, ph, bl: (b, h, 0, 0))
    grid_spec = pltpu.PrefetchScalarGridSpec(
        num_scalar_prefetch=2, grid=(n_batch, N_HEADS),
        in_specs=[vec, vec, vec]
        + [pl.BlockSpec((1, PAGE_SIZE, hd), pg_map(i)) for i in range(n_pg)]
        + [pl.BlockSpec((1, PAGE_SIZE, hd), pg_map(i)) for i in range(n_pg)]
        + [pl.BlockSpec((1, 8, MOBA_BLOCK), lambda b, h, ph, bl: (h, 0, 0))],
        out_specs=vec)
    return pl.pallas_call(
        functools.partial(_decode_body, last_block=last_block), grid_spec=grid_spec,
        out_shape=jax.ShapeDtypeStruct((n_batch, N_HEADS, 1, hd), F32),
        compiler_params=_cparams(("arbitrary", "arbitrary"), 32), name="decode")(
            phys, blks, q4, kn4, vn4, *([kpool] * n_pg), *([vpool] * n_pg), tabs)


def _sconv_body(p_ref, cs_ref, cw_ref, cb_ref, dtb_ref, alog_ref, x_ref, b_ref, c_ref, dt_ref, da_ref, nc_ref):
    xbc = p_ref[:, COL_X:COL_X + CONV_DIM]
    cs = cs_ref[...]
    conv = cb_ref[...] + xbc * cw_ref[CONV_WIDTH - 1:CONV_WIDTH, :]
    for i in range(CONV_WIDTH - 1):
        conv = conv + cs[:, i * CONV_DIM:(i + 1) * CONV_DIM] * cw_ref[i:i + 1, :]
    nc_ref[:, 0:(CONV_WIDTH - 2) * CONV_DIM] = cs[:, CONV_DIM:]
    nc_ref[:, (CONV_WIDTH - 2) * CONV_DIM:] = xbc
    act = _silu(conv)
    gn = N_SSD_GROUPS * SSD_STATE
    x_ref[...] = act[:, :SSD_WIDTH]
    b_ref[...] = act[:, SSD_WIDTH:SSD_WIDTH + gn]
    c_ref[...] = act[:, SSD_WIDTH + gn:]
    dtv = p_ref[:, COL_DT:COL_DT + LANES] + dtb_ref[...]
    dt = jnp.maximum(dtv, 0.0) + jnp.log1p(jnp.exp(-jnp.abs(dtv)))
    dt_ref[...] = dt
    da_ref[...] = jnp.exp(dt * (-jnp.exp(alog_ref[...])))


def _sconv(proj_s, conv_state2, conv_w, conv_b, dt_bias_p, a_log_p):
    n = proj_s.shape[0]
    gn = N_SSD_GROUPS * SSD_STATE
    shapes = [(n, SSD_WIDTH), (n, gn), (n, gn), (n, LANES), (n, LANES), (n, (CONV_WIDTH - 1) * CONV_DIM)]
    return pl.pallas_call(
        _sconv_body,
        out_shape=[jax.ShapeDtypeStruct(s, F32) for s in shapes],
        compiler_params=pltpu.CompilerParams(vmem_limit_bytes=32 * 1024 * 1024), name="sconv")(
            proj_s, conv_state2, conv_w, conv_b, dt_bias_p, a_log_p)


def _sstate_body(st_ref, x_ref, dt_ref, da_ref, b_ref, c_ref, ns_ref, y_ref):
    hpg = N_SSD_HEADS // N_SSD_GROUPS
    for g in range(N_SSD_GROUPS):
        hs = slice(g * hpg, (g + 1) * hpg)
        st = st_ref[0, hs]
        xdt = x_ref[0, hs] * dt_ref[0, hs]
        ns = st * da_ref[0, hs] + xdt * b_ref[0, g]
        ns_ref[0, hs] = ns
        y_ref[0, hs] = jnp.sum(ns * c_ref[0, g], axis=-1, keepdims=True)


def _sstate(state, x4, dt4, da4, b4, c4):
    n = state.shape[0]
    h, p, s = N_SSD_HEADS, SSD_HEAD_DIM, SSD_STATE
    g = N_SSD_GROUPS
    m4 = lambda b: (b, 0, 0, 0)
    return pl.pallas_call(
        _sstate_body, grid=(n,),
        in_specs=[pl.BlockSpec((1, h, p, s), m4), pl.BlockSpec((1, h, p, 1), m4),
                  pl.BlockSpec((1, h, 1, 1), m4), pl.BlockSpec((1, h, 1, 1), m4),
                  pl.BlockSpec((1, g, 1, s), m4), pl.BlockSpec((1, g, 1, s), m4)],
        out_specs=[pl.BlockSpec((1, h, p, s), m4), pl.BlockSpec((1, h, p, 1), m4)],
        out_shape=[jax.ShapeDtypeStruct((n, h, p, s), F32), jax.ShapeDtypeStruct((n, h, p, 1), F32)],
        compiler_params=_cparams(("arbitrary",), 32), name="sstate")(state, x4, dt4, da4, b4, c4)


def _sgate_body(y_ref, x_ref, p_ref, dsk_ref, nrm_ref, o_ref):
    y = y_ref[...] + dsk_ref[...] * x_ref[...]
    y = y * _silu(p_ref[:, COL_Z:COL_Z + SSD_WIDTH])
    r = lax.rsqrt(jnp.mean(y * y, axis=-1, keepdims=True) + RMS_EPS)
    o_ref[...] = ((y * r) * nrm_ref[...]).astype(o_ref.dtype)


def _sgate(y, xs, proj_s, dskip_w, ssd_norm):
    return pl.pallas_call(
        _sgate_body, out_shape=jax.ShapeDtypeStruct(y.shape, BF16),
        compiler_params=pltpu.CompilerParams(vmem_limit_bytes=32 * 1024 * 1024), name="sgate")(
            y, xs, proj_s, dskip_w, ssd_norm)


def kernel(x_prompt, x_sample, cache_k, cache_v, state_ssm, state_conv, page_table, c_prompt, c_sample,
           w_ada, b_ada, norm_mix, norm_ffn, w_in, conv_w, conv_b, dt_bias, a_log, d_skip, ssd_norm,
           w_out, rel_bias, w_router, router_bias, w_gate_e, w_up_e, w_down_e, w_gate_s, w_up_s, w_down_s,
           norm_final):
    depth = w_ada.shape[0]
    assert depth == 1
    bp, sp, d = x_prompt.shape
    bs, sn, _ = x_sample.shape
    assert bp == 1 and sn == 1 and d == D_MODEL
    n_pages = page_table.shape[1]
    past_len = n_pages * PAGE_SIZE
    nb_prompt = sp // MOBA_BLOCK
    nb_past = past_len // MOBA_BLOCK
    l = 0
    layer = lambda a: a.reshape(a.shape[1:])

    w_in_b = jnp.pad(layer(w_in).astype(BF16), ((0, 0), (0, IN_COLS_PAD - IN_COLS)))
    w_out_b = layer(w_out).astype(BF16)
    wgs_b, wus_b, wds_b = (layer(w).astype(BF16) for w in (w_gate_s, w_up_s, w_down_s))
    wr2 = jnp.concatenate([layer(w_router), layer(w_router)], axis=1)
    wr_hi = wr2.astype(BF16)
    wr_lo = (wr2 - wr_hi.astype(F32)).astype(BF16)
    rb2 = jnp.concatenate([router_bias[l], router_bias[l]])[None, :].astype(F32)
    g_mix = norm_mix[l][None, :]
    g_ffn = norm_ffn[l][None, :]
    g_fin = norm_final[None, :]
    pad_h = LANES - N_SSD_HEADS
    dt_bias_p = jnp.pad(dt_bias[l].astype(F32), (0, pad_h))[None, :]
    a_log_p = jnp.pad(a_log[l].astype(F32), (0, pad_h))[None, :]
    dskip_w = jnp.repeat(d_skip[l].astype(F32), SSD_HEAD_DIM)[None, :]
    nrm_ssd = ssd_norm[l][None, :]
    cw = conv_w[l]
    cbias = conv_b[l][None, :]
    expand = _head_expand()
    expand_t = expand.T
    own_bias, prev_bias = _prompt_bias_tables(rel_bias)
    rel_t = rel_bias.T

    c_all = jnp.concatenate([c_prompt, jnp.zeros((8 - bp, d), F32), c_sample], axis=0)
    mod = _adaln(c_all, layer(w_ada), b_ada[l][None, :])
    mod_p = mod[0:1]
    mod_s = mod[8:8 + bs]

    xp = x_prompt.reshape(sp, d)
    proj = _inproj(xp, g_mix, mod_p, w_in_b, 1024)
    k_p, v_p, kaug, v_t, kmean = _kvpost(proj)
    kmean_pad = jnp.pad(kmean.reshape(nb_prompt, ATT_WIDTH), ((0, LANES - nb_prompt), (0, 0)))
    att = _moba_prompt(proj, kaug, v_t, kmean_pad, own_bias, prev_bias)
    ssd, ssm_p, conv_tail = _ssd_prompt(proj, cw, cbias, dt_bias_p, a_log_p, dskip_w, nrm_ssd, expand, expand_t)
    xp1 = _outproj(att, ssd, w_out_b, xp, mod_p, 1024)

    xs = x_sample.reshape(bs, d)
    proj_s = _inproj(xs, g_mix, mod_s, w_in_b, bs)
    q_s = proj_s[:, COL_Q:COL_Q + ATT_WIDTH]
    k_s = proj_s[:, COL_K:COL_K + ATT_WIDTH]
    v_s = proj_s[:, COL_V:COL_V + ATT_WIDTH]
    kpool = cache_k.reshape(cache_k.shape[1], PAGE_SIZE, ATT_WIDTH)
    vpool = cache_v.reshape(cache_v.shape[1], PAGE_SIZE, ATT_WIDTH)
    ksum = _pagesum(page_table.reshape(-1), kpool, bs, n_pages)
    sel = _gate3(q_s.reshape(bs, 1, ATT_WIDTH), ksum)
    blks = jnp.transpose(sel[:, :MOBA_TOPK, :], (0, 2, 1))
    ppb = MOBA_BLOCK // PAGE_SIZE
    lpage = blks[..., None] * ppb + jnp.arange(ppb, dtype=I32)
    phys = jnp.take_along_axis(page_table, lpage.reshape(bs, -1), axis=1)
    dist_last = past_len - ((nb_past - 1) * MOBA_BLOCK + jnp.arange(MOBA_BLOCK))
    tab_last = rel_t[:, _rel_bucket(dist_last)]
    tab_far = jnp.broadcast_to(rel_t[:, NUM_BUCKETS - 1:NUM_BUCKETS], tab_last.shape)
    tab_self = jnp.broadcast_to(rel_t[:, 0:1], tab_last.shape)
    tabs = jnp.stack([tab_last, tab_far, tab_self] + [jnp.zeros_like(tab_last)] * 5, axis=1).astype(F32)
    shape4 = (bs, N_HEADS, 1, HEAD_DIM)
    att_s = _decode(phys.reshape(-1).astype(I32), blks.reshape(-1).astype(I32), q_s.reshape(shape4),
                    k_s.reshape(shape4), v_s.reshape(shape4), kpool, vpool, tabs, nb_past - 1)
    att_s = att_s.reshape(bs, ATT_WIDTH).astype(BF16)

    xs_act, b_s, c_s, dt_s, da_s, conv_s = _sconv(
        proj_s, state_conv.reshape(bs, (CONV_WIDTH - 1) * CONV_DIM), cw, cbias, dt_bias_p, a_log_p)
    ssm_s, y_s = _sstate(
        layer(state_ssm), xs_act.reshape(bs, N_SSD_HEADS, SSD_HEAD_DIM, 1),
        dt_s[:, :N_SSD_HEADS].reshape(bs, N_SSD_HEADS, 1, 1), da_s[:, :N_SSD_HEADS].reshape(bs, N_SSD_HEADS, 1, 1),
        b_s.reshape(bs, N_SSD_GROUPS, 1, SSD_STATE), c_s.reshape(bs, N_SSD_GROUPS, 1, SSD_STATE))
    ssd_s = _sgate(y_s.reshape(bs, SSD_WIDTH), xs_act, proj_s, dskip_w, nrm_ssd)
    xs1 = _outproj(att_s, ssd_s, w_out_b, xs, mod_s, bs)

    n_tok = sp + bs
    n_assign = n_tok * TOP_K
    ffn_tm = 256
    assert sp % ffn_tm == 0 and sp % bs == 0 and bs <= ffn_tm
    cnt0 = jnp.zeros((1, LANES), F32)
    hp, idx_p, wt_p, rank_p, cnt_p = _ffnpre(xp1, g_ffn, mod_p, wr_hi, wr_lo, rb2, cnt0, ffn_tm,
                                             hp_tok=sp + ffn_tm)
    hp, idx_s, wt_s, rank_s, cnt_all = _ffnpre(xs1, g_ffn, mod_s, wr_hi, wr_lo, rb2, cnt_p[0:1], bs,
                                               hp_prev=hp, hp_tok=(sp, bs))
    idx = jnp.concatenate([idx_p[:, :TOP_K], idx_s[:, :TOP_K]], axis=0)
    rank = jnp.concatenate([rank_p[:, :TOP_K], rank_s[:, :TOP_K]], axis=0)
    code = ((idx << RANK_BITS) | rank).reshape(-1)
    blk_e, n_live, pstart, counts, n_rows = _expert_layout(cnt_all, n_assign)
    x_sorted = _route(code, pstart, counts, hp, n_assign, n_rows)
    y_sorted = _moe(blk_e, n_live, x_sorted, layer(w_gate_e), layer(w_up_e), layer(w_down_e))
    y_p = _combine(code, pstart, xp1, y_sorted, hp, wt_p, wgs_b, wus_b, wds_b, mod_p, g_fin, 128, 0)
    y_s_out = _combine(code, pstart, xs1, y_sorted, hp, wt_s, wgs_b, wus_b, wds_b, mod_s, g_fin, bs, sp)

    kv_shape = (1, bp, sp, N_HEADS, HEAD_DIM)
    kvs_shape = (1, bs, sn, N_HEADS, HEAD_DIM)
    return (y_p.reshape(bp, sp, d), y_s_out.reshape(bs, sn, d),
            k_p.reshape(kv_shape), v_p.reshape(kv_shape),
            k_s.reshape(kvs_shape), v_s.reshape(kvs_shape),
            ssm_p.reshape(1, bp, N_SSD_HEADS, SSD_HEAD_DIM, SSD_STATE),
            ssm_s.reshape(1, bs, N_SSD_HEADS, SSD_HEAD_DIM, SSD_STATE),
            conv_tail[8 - (CONV_WIDTH - 1):].reshape(1, bp, CONV_WIDTH - 1, CONV_DIM),
            conv_s.reshape(1, bs, CONV_WIDTH - 1, CONV_DIM))
```

```python
import functools
import math

import jax
import jax.numpy as jnp
from jax import lax
from jax.experimental import pallas as pl
from jax.experimental.pallas import tpu as pltpu

F32 = jnp.float32
BF16 = jnp.bfloat16
I32 = jnp.int32
U32 = jnp.uint32

D_MODEL = 2048
ATT_WIDTH = 1024
HEAD_DIM = 128
N_HEADS = 8
MOBA_BLOCK = 256
MOBA_TOPK = 3
Q_BLOCK = 128
NUM_BUCKETS = 32
MAX_DISTANCE = 128
SSD_WIDTH = 1024
SSD_HEAD_DIM = 64
N_SSD_HEADS = 16
N_SSD_GROUPS = 2
SSD_STATE = 128
CONV_WIDTH = 4
SSD_CHUNK = 256
CONV_DIM = SSD_WIDTH + 2 * N_SSD_GROUPS * SSD_STATE
N_EXPERTS = 64
TOP_K = 8
N_EXPERT_GROUPS = 8
TOPK_GROUPS = 4
D_EXPERT = 512
ROUTED_SCALE = 2.5
RMS_EPS = 1e-6
PAGE_SIZE = 128

LANES = 128
TILE_ROWS = 8
LOG2E = 1.4426950408889634
MOBA_GROUP = 8
COL_Q, COL_K, COL_V, COL_Z, COL_X = 0, 1024, 2048, 3072, 4096
COL_B, COL_C, COL_DT = 5120, 5376, 5632
IN_COLS = 5648
IN_COLS_PAD = 6144
PROJ_TN = 512
MOE_ROWS = 256
MASK_BIG = 2.0 ** 40
NT_DIMS = (((1,), (1,)), ((), ()))


def _cparams(sem, vmem_mb):
    return pltpu.CompilerParams(dimension_semantics=sem, vmem_limit_bytes=vmem_mb * 1024 * 1024)


def _silu(x):
    return x * jax.nn.sigmoid(x)


def _norm_mod(x, g, shift, scale):
    r = lax.rsqrt(jnp.mean(x * x, axis=-1, keepdims=True) + RMS_EPS)
    return (x * r) * g * (1.0 + scale) + shift


def _split2(x):
    hi = x.astype(BF16)
    lo = (x - hi.astype(F32)).astype(BF16)
    return hi, lo


def _split3(x):
    h1 = x.astype(BF16)
    r1 = x - h1.astype(F32)
    h2 = r1.astype(BF16)
    h3 = (r1 - h2.astype(F32)).astype(BF16)
    return h1, h2, h3


def _pack_pair(lo, hi):
    lo_b = lax.bitcast_convert_type(lo.astype(BF16).astype(F32), U32) >> 16
    hi_b = lax.bitcast_convert_type(hi.astype(BF16).astype(F32), U32) & jnp.uint32(0xFFFF0000)
    return lo_b | hi_b


def _unpack_pair(u):
    lo = lax.bitcast_convert_type(u << 16, F32)
    hi = lax.bitcast_convert_type(u & jnp.uint32(0xFFFF0000), F32)
    return lo, hi


def _adaln_body(c_ref, w_ref, b_ref, o_ref):
    s = _silu(c_ref[...])
    o_ref[...] = jnp.dot(s.astype(BF16), w_ref[...].astype(BF16), preferred_element_type=F32) + b_ref[...]


def _adaln(c_all, w, b):
    r, d = c_all.shape
    n = w.shape[1]
    tn = 1024
    return pl.pallas_call(
        _adaln_body, grid=(n // tn,),
        in_specs=[pl.BlockSpec((r, d), lambda j: (0, 0)),
                  pl.BlockSpec((d, tn), lambda j: (0, j)),
                  pl.BlockSpec((1, tn), lambda j: (0, j))],
        out_specs=pl.BlockSpec((r, tn), lambda j: (0, j)),
        out_shape=jax.ShapeDtypeStruct((r, n), F32),
        compiler_params=_cparams(("arbitrary",), 40), name="adaln")(c_all, w, b)


def _inproj_body(x_ref, g_ref, sh_ref, sc_ref, w_ref, o_ref, h_scr):
    @pl.when(pl.program_id(1) == 0)
    def _():
        h_scr[...] = _norm_mod(x_ref[...], g_ref[...], sh_ref[...], sc_ref[...]).astype(BF16)

    o_ref[...] = jnp.dot(h_scr[...], w_ref[...], preferred_element_type=F32)


def _inproj(x, g, mod, w_pad, tm):
    t, d = x.shape
    r = mod.shape[0]
    n = w_pad.shape[1]
    return pl.pallas_call(
        _inproj_body, grid=(t // tm, n // PROJ_TN),
        in_specs=[pl.BlockSpec((tm, d), lambda i, j: (i, 0)),
                  pl.BlockSpec((1, d), lambda i, j: (0, 0)),
                  pl.BlockSpec((r, d), lambda i, j: (0, 0)),
                  pl.BlockSpec((r, d), lambda i, j: (0, 1)),
                  pl.BlockSpec((d, PROJ_TN), lambda i, j: (0, j))],
        out_specs=pl.BlockSpec((tm, PROJ_TN), lambda i, j: (i, j)),
        out_shape=jax.ShapeDtypeStruct((t, n), F32),
        scratch_shapes=[pltpu.VMEM((tm, d), BF16)],
        compiler_params=_cparams(("arbitrary", "arbitrary"), 48), name="inproj")(x, g, mod, mod, w_pad)


def _kvpost_body(k_ref, v_ref, ko, vo, kao, vto, kmo):
    k = k_ref[...]
    v = v_ref[...]
    ko[...] = k
    vo[...] = v
    rows = k.shape[0]
    row = lax.broadcasted_iota(I32, (rows, LANES), 0)
    lane = lax.broadcasted_iota(I32, (rows, LANES), 1)
    blk = pl.program_id(0) * (rows // MOBA_BLOCK) + row // MOBA_BLOCK
    onehot = jnp.where(lane == blk, 1.0, 0.0).astype(BF16)
    kb = k.astype(BF16)
    pieces = []
    for h in range(N_HEADS):
        pieces += [kb[:, h * HEAD_DIM:(h + 1) * HEAD_DIM], onehot]
    kao[...] = jnp.concatenate(pieces, axis=1)
    vto[...] = v.T.astype(BF16)
    for i in range(rows // MOBA_BLOCK):
        kmo[i] = jnp.sum(k[i * MOBA_BLOCK:(i + 1) * MOBA_BLOCK], axis=0, keepdims=True) * (1.0 / MOBA_BLOCK)


def _kvpost(proj):
    t = proj.shape[0]
    r = 512
    w = ATT_WIDTH
    nb = t // MOBA_BLOCK
    return pl.pallas_call(
        _kvpost_body, grid=(t // r,),
        in_specs=[pl.BlockSpec((r, w), lambda i: (i, COL_K // w)),
                  pl.BlockSpec((r, w), lambda i: (i, COL_V // w))],
        out_specs=[pl.BlockSpec((r, w), lambda i: (i, 0)),
                   pl.BlockSpec((r, w), lambda i: (i, 0)),
                   pl.BlockSpec((r, 2 * w), lambda i: (i, 0)),
                   pl.BlockSpec((w, r), lambda i: (0, i)),
                   pl.BlockSpec((r // MOBA_BLOCK, 1, w), lambda i: (i, 0, 0))],
        out_shape=[jax.ShapeDtypeStruct((t, w), F32), jax.ShapeDtypeStruct((t, w), F32),
                   jax.ShapeDtypeStruct((t, 2 * w), BF16), jax.ShapeDtypeStruct((w, t), BF16),
                   jax.ShapeDtypeStruct((nb, 1, w), F32)],
        compiler_params=_cparams(("arbitrary",), 40), name="kvpost")(proj, proj)


def _rel_bucket(dist):
    d = jnp.maximum(dist, 0)
    max_exact = NUM_BUCKETS // 2
    ratio = jnp.log(jnp.maximum(d, 1).astype(F32) / max_exact) / math.log(MAX_DISTANCE / max_exact)
    large = jnp.minimum(max_exact + (ratio * (NUM_BUCKETS - max_exact)).astype(I32), NUM_BUCKETS - 1)
    return jnp.where(d < max_exact, d, large)


def _bias_lookup(rel_t, bucket):
    out = jnp.zeros((rel_t.shape[0],) + bucket.shape, F32)
    for b in range(NUM_BUCKETS):
        out = jnp.where(bucket[None] == b, rel_t[:, b].reshape((-1,) + (1,) * bucket.ndim), out)
    return out


def _prompt_bias_tables(rel_bias):
    rel_t = rel_bias.T
    far = rel_t[:, NUM_BUCKETS - 1][:, None, None]
    j = jnp.arange(MOBA_BLOCK)[:, None]
    i = jnp.arange(MOBA_BLOCK)[None, :]
    d_own = i - j
    own = jnp.where(d_own >= 0, (_bias_lookup(rel_t, _rel_bucket(d_own)) - far) * LOG2E, -MASK_BIG)
    prev = (_bias_lookup(rel_t, _rel_bucket(d_own + MOBA_BLOCK)) - far) * LOG2E
    return own.astype(F32), prev.astype(F32)


def _moba_body(q_ref, ka_ref, vt_ref, km_ref, ob_ref, pb_ref, o_ref):
    own = pl.program_id(1)
    nq = MOBA_BLOCK
    hd = HEAD_DIM
    dotf = functools.partial(jnp.dot, preferred_element_type=F32)
    q_t = q_ref[...].T
    q_hi, q_lo = _split2(q_t)
    km_hi, km_lo = _split2(km_ref[...])
    gate = dotf(km_hi, q_hi) + dotf(km_hi, q_lo) + dotf(km_lo, q_hi)
    rowi = lax.broadcasted_iota(I32, gate.shape, 0)
    row_f = rowi.astype(F32)
    g = jnp.where(rowi < own, gate, -jnp.inf)
    sel = jnp.zeros(gate.shape, F32)
    for _ in range(MOBA_TOPK):
        m = jnp.max(g, axis=0, keepdims=True)
        idx = jnp.min(jnp.where(g == m, row_f, 1e9), axis=0, keepdims=True)
        pick = row_f == idx
        sel = jnp.where(pick, 1.0, sel)
        g = jnp.where(pick, -jnp.inf, g)
    sel = jnp.where(rowi < own, sel, 0.0)
    mq_t = jnp.where(jnp.where(rowi < own - 1, sel, 0.0) > 0.0, 0.0, -MASK_BIG).astype(BF16)
    qs_t = (q_t * (hd ** -0.5 * LOG2E)).astype(BF16)
    q_aug_t = jnp.concatenate([qs_t, mq_t], axis=0)

    prev = jnp.maximum(own - 1, 0)
    st_own = pl.multiple_of(own * MOBA_BLOCK, MOBA_BLOCK)
    st_prev = pl.multiple_of(prev * MOBA_BLOCK, MOBA_BLOCK)
    s_own = dotf(ka_ref[pl.ds(st_own, MOBA_BLOCK), 0:hd], qs_t) + ob_ref[0]
    sel_prev = jnp.max(jnp.where(rowi == own - 1, sel, 0.0), axis=0, keepdims=True)
    s_prev = dotf(ka_ref[pl.ds(st_prev, MOBA_BLOCK), 0:hd], qs_t) + pb_ref[0] \
        + jnp.where(sel_prev > 0.0, 0.0, -MASK_BIG)
    m0 = jnp.maximum(jnp.max(s_own, axis=0, keepdims=True), jnp.max(s_prev, axis=0, keepdims=True))
    p_own = jnp.exp2(s_own - m0)
    p_prev = jnp.exp2(s_prev - m0)
    l0 = jnp.sum(p_own, axis=0, keepdims=True) + jnp.sum(p_prev, axis=0, keepdims=True)
    acc0 = dotf(vt_ref[:, pl.ds(st_own, MOBA_BLOCK)], p_own.astype(BF16)) \
        + dotf(vt_ref[:, pl.ds(st_prev, MOBA_BLOCK)], p_prev.astype(BF16))

    span = MOBA_GROUP * MOBA_BLOCK

    def group(it, carry):
        m, l, acc = carry
        st = pl.multiple_of(it * span, span)
        s = dotf(ka_ref[pl.ds(st, span), :], q_aug_t)
        m_new = jnp.maximum(m, jnp.max(s, axis=0, keepdims=True))
        p = jnp.exp2(s - m_new)
        alpha = jnp.exp2(m - m_new)
        l = alpha * l + jnp.sum(p, axis=0, keepdims=True)
        acc = alpha * acc + dotf(vt_ref[:, pl.ds(st, span)], p.astype(BF16))
        return m_new, l, acc

    n_groups = (jnp.maximum(own - 1, 0) + MOBA_GROUP - 1) // MOBA_GROUP
    _, l, acc = lax.fori_loop(0, n_groups, group, (m0, l0, acc0))
    o_ref[...] = (acc * (1.0 / l)).T.astype(o_ref.dtype)


def _moba_prompt(proj, kaug, v_t, kmean_pad, own_bias, prev_bias):
    t = proj.shape[0]
    hd = HEAD_DIM
    nb = t // MOBA_BLOCK
    assert nb % MOBA_GROUP == 0 and nb <= LANES
    return pl.pallas_call(
        _moba_body, grid=(N_HEADS, nb),
        in_specs=[pl.BlockSpec((MOBA_BLOCK, hd), lambda h, c: (c, h)),
                  pl.BlockSpec((t, 2 * hd), lambda h, c: (0, h)),
                  pl.BlockSpec((hd, t), lambda h, c: (h, 0)),
                  pl.BlockSpec((LANES, hd), lambda h, c: (0, h)),
                  pl.BlockSpec((1, MOBA_BLOCK, MOBA_BLOCK), lambda h, c: (h, 0, 0)),
                  pl.BlockSpec((1, MOBA_BLOCK, MOBA_BLOCK), lambda h, c: (h, 0, 0))],
        out_specs=pl.BlockSpec((MOBA_BLOCK, hd), lambda h, c: (c, h)),
        out_shape=jax.ShapeDtypeStruct((t, ATT_WIDTH), BF16),
        compiler_params=_cparams(("arbitrary", "arbitrary"), 56), name="moba_prompt")(
            proj, kaug, v_t, kmean_pad, own_bias, prev_bias)


def _ssd_body(z_ref, x_ref, b_ref, c_ref, dt_ref, cw_ref, cb_ref, dtb_ref, alog_ref, dsk_ref, nrm_ref,
              exp_ref, expt_ref, y_ref, st_ref, tail_ref, xpad, state):
    i = pl.program_id(0)
    L = SSD_CHUNK
    W = SSD_WIDTH
    GN = N_SSD_GROUPS * SSD_STATE

    @pl.when(i == 0)
    def _():
        xpad[0:8, :] = jnp.zeros((8, CONV_DIM), F32)
        state[...] = jnp.zeros(state.shape, F32)

    xpad[8:8 + L, 0:W] = x_ref[...]
    xpad[8:8 + L, W:W + GN] = b_ref[...]
    xpad[8:8 + L, W + GN:W + 2 * GN] = c_ref[...]
    conv = cb_ref[...] + xpad[5:5 + L, :] * cw_ref[0:1, :]
    for r in range(1, CONV_WIDTH):
        conv = conv + xpad[5 + r:5 + r + L, :] * cw_ref[r:r + 1, :]
    tail = xpad[L:L + 8, :]
    tail_ref[...] = tail
    xpad[0:8, :] = tail
    act = _silu(conv)
    xs = act[:, 0:W]
    bm = act[:, W:W + GN].astype(BF16)
    cm = act[:, W + GN:W + 2 * GN].astype(BF16)

    dtv = dt_ref[...] + dtb_ref[...]
    dt = jnp.maximum(dtv, 0.0) + jnp.log1p(jnp.exp(-jnp.abs(dtv)))
    a = -jnp.exp(alog_ref[...])
    adt = dt * a
    row = lax.broadcasted_iota(I32, (L, L), 0)
    col = lax.broadcasted_iota(I32, (L, L), 1)
    causal = row >= col
    tril = jnp.where(causal, 1.0, 0.0).astype(BF16)
    dotf = functools.partial(jnp.dot, preferred_element_type=F32)
    acum = sum(dotf(tril, part) for part in _split3(adt))
    acum_t = acum.T
    expand = exp_ref[...]

    def widen(v):
        return sum(dotf(part, expand) for part in _split3(v))

    xd = xs * widen(dt)
    xd_b = xd.astype(BF16)
    lane = lax.broadcasted_iota(I32, (L, LANES), 1)
    low_half = lane < SSD_HEAD_DIM
    dg = functools.partial(lax.dot_general, dimension_numbers=NT_DIMS, preferred_element_type=F32)
    hpg = N_SSD_HEADS // N_SSD_GROUPS
    y_cols = []
    cbs = [dg(cm[:, g * SSD_STATE:(g + 1) * SSD_STATE], bm[:, g * SSD_STATE:(g + 1) * SSD_STATE])
           for g in range(N_SSD_GROUPS)]
    for pair in range(N_SSD_HEADS // 2):
        cb = cbs[(2 * pair) // hpg]
        xj = xd_b[:, pair * LANES:(pair + 1) * LANES]
        acc = None
        for sub in range(2):
            h = 2 * pair + sub
            diff = acum[:, h:h + 1] - acum_t[h:h + 1, :]
            dec = jnp.exp(jnp.where(causal, diff, -jnp.inf))
            mh = (cb * dec).astype(BF16)
            xh = jnp.where(low_half if sub == 0 else jnp.logical_not(low_half), xj, jnp.zeros_like(xj))
            part = dotf(mh, xh)
            acc = part if acc is None else acc + part
        y_cols.append(acc)
    y_diag = jnp.concatenate(y_cols, axis=1)

    s_prev = state[...]
    s_b = s_prev.astype(BF16)
    half_w = W // N_SSD_GROUPS
    y_off = jnp.concatenate(
        [dg(cm[:, g * SSD_STATE:(g + 1) * SSD_STATE], s_b[g * half_w:(g + 1) * half_w, :])
         for g in range(N_SSD_GROUPS)], axis=1) * widen(jnp.exp(acum))

    dte = jnp.exp(acum[L - 1:L, :] - acum)
    xdd = xd * widen(dte)
    upd = jnp.concatenate(
        [dotf(xdd[:, g * half_w:(g + 1) * half_w].T.astype(BF16), bm[:, g * SSD_STATE:(g + 1) * SSD_STATE])
         for g in range(N_SSD_GROUPS)], axis=0)
    last = jnp.broadcast_to(acum_t[:, L - 1:L], (LANES, SSD_STATE))
    dec_rows = jnp.exp(sum(dotf(expt_ref[...], part) for part in _split3(last)))
    s_new = s_prev * dec_rows + upd
    state[...] = s_new
    st_ref[...] = s_new

    y = y_diag + y_off + dsk_ref[...] * xs
    y = y * _silu(z_ref[...])
    r = lax.rsqrt(jnp.mean(y * y, axis=-1, keepdims=True) + RMS_EPS)
    y_ref[...] = ((y * r) * nrm_ref[...]).astype(y_ref.dtype)


def _head_expand():
    h = jnp.arange(LANES)[:, None]
    ch = jnp.arange(SSD_WIDTH)[None, :]
    return jnp.where(ch // SSD_HEAD_DIM == h, 1.0, 0.0).astype(BF16)


def _ssd_prompt(proj, conv_w, conv_b, dt_bias_p, a_log_p, dskip_w, ssd_norm, expand, expand_t):
    t = proj.shape[0]
    L = SSD_CHUNK
    W = SSD_WIDTH
    GN = N_SSD_GROUPS * SSD_STATE
    const = lambda shape: pl.BlockSpec(shape, lambda i: (0, 0))
    return pl.pallas_call(
        _ssd_body, grid=(t // L,),
        in_specs=[pl.BlockSpec((L, W), lambda i: (i, COL_Z // W)),
                  pl.BlockSpec((L, W), lambda i: (i, COL_X // W)),
                  pl.BlockSpec((L, GN), lambda i: (i, COL_B // GN)),
                  pl.BlockSpec((L, GN), lambda i: (i, COL_C // GN)),
                  pl.BlockSpec((L, LANES), lambda i: (i, COL_DT // LANES)),
                  const((CONV_WIDTH, CONV_DIM)), const((1, CONV_DIM)), const((1, LANES)), const((1, LANES)),
                  const((1, W)), const((1, W)), const((LANES, W)), const((W, LANES))],
        out_specs=[pl.BlockSpec((L, W), lambda i: (i, 0)),
                   pl.BlockSpec((W, SSD_STATE), lambda i: (0, 0)),
                   pl.BlockSpec((8, CONV_DIM), lambda i: (0, 0))],
        out_shape=[jax.ShapeDtypeStruct((t, W), BF16),
                   jax.ShapeDtypeStruct((W, SSD_STATE), F32),
                   jax.ShapeDtypeStruct((8, CONV_DIM), F32)],
        scratch_shapes=[pltpu.VMEM((L + 8, CONV_DIM), F32), pltpu.VMEM((W, SSD_STATE), F32)],
        compiler_params=_cparams(("arbitrary",), 40), name="ssd_prompt")(
            proj, proj, proj, proj, proj, conv_w, conv_b, dt_bias_p, a_log_p, dskip_w, ssd_norm, expand, expand_t)


def _outproj_body(a_ref, s_ref, w_ref, x_ref, g_ref, o_ref):
    w = ATT_WIDTH
    acc = jnp.dot(a_ref[...], w_ref[0:w, :], preferred_element_type=F32)
    acc = acc + jnp.dot(s_ref[...], w_ref[w:, :], preferred_element_type=F32)
    o_ref[...] = x_ref[...] + g_ref[...] * acc


def _outproj(att, ssd, w_out_b, x, mod, tm):
    t, d = x.shape
    r = mod.shape[0]
    tn = PROJ_TN
    return pl.pallas_call(
        _outproj_body, grid=(t // tm, d // tn),
        in_specs=[pl.BlockSpec((tm, ATT_WIDTH), lambda i, j: (i, 0)),
                  pl.BlockSpec((tm, SSD_WIDTH), lambda i, j: (i, 0)),
                  pl.BlockSpec((ATT_WIDTH + SSD_WIDTH, tn), lambda i, j: (0, j)),
                  pl.BlockSpec((tm, tn), lambda i, j: (i, j)),
                  pl.BlockSpec((r, tn), lambda i, j: (0, 2 * (d // tn) + j))],
        out_specs=pl.BlockSpec((tm, tn), lambda i, j: (i, j)),
        out_shape=jax.ShapeDtypeStruct((t, d), F32),
        compiler_params=_cparams(("arbitrary", "arbitrary"), 40), name="outproj")(att, ssd, w_out_b, x, mod)


def _group_allreduce(v, lane, op):
    for s in (1, 2, 4):
        up = pltpu.roll(v, LANES - s, axis=1)
        dn = pltpu.roll(v, s, axis=1)
        v = op(v, jnp.where((lane & s) == 0, up, dn))
    return v


def _store_token_tiles(ref, packed, n_tok):
    for s in range(TILE_ROWS):
        ref[pl.ds(s, n_tok, stride=TILE_ROWS), :] = packed[:, s * LANES:(s + 1) * LANES]


def _load_token_tiles(ref, n_tok, start=0, stride=TILE_ROWS):
    return jnp.concatenate([ref[pl.ds(start + s, n_tok, stride=stride), :] for s in range(TILE_ROWS)], axis=1)


def _ffnpre_body(x_ref, g_ref, sh_ref, sc_ref, wrh_ref, wrl_ref, rb_ref, cin_ref,
                 hp_ref, idx_ref, wt_ref, rank_ref, cnt_ref, cnt_scr, *, rows, n_real):
    i = pl.program_id(0)
    real = i < n_real

    @pl.when(i == 0)
    def _():
        cnt_scr[...] = cin_ref[...]

    h = _norm_mod(x_ref[...], g_ref[...], sh_ref[...], sc_ref[...])
    half = D_MODEL // 2
    packed = _pack_pair(h[:, :half], h[:, half:])
    _store_token_tiles(hp_ref, jnp.where(real, packed, jnp.zeros_like(packed)), rows)
    if rows * TILE_ROWS != hp_ref.shape[0]:
        hp_ref[rows * TILE_ROWS:, :] = jnp.zeros((hp_ref.shape[0] - rows * TILE_ROWS, LANES), U32)
    h_hi, h_lo = _split2(h)
    dotf = functools.partial(jnp.dot, preferred_element_type=F32)
    logits = dotf(h_hi, wrh_ref[...]) + dotf(h_hi, wrl_ref[...]) + dotf(h_lo, wrh_ref[...])
    scores = jax.nn.sigmoid(logits)
    biased = scores + rb_ref[...]
    lane = lax.broadcasted_iota(I32, biased.shape, 1)
    lane_f = lane.astype(F32)
    per_group = N_EXPERTS // N_EXPERT_GROUPS
    m1 = _group_allreduce(biased, lane, jnp.maximum)
    i1 = _group_allreduce(jnp.where(biased == m1, lane_f, 1e9), lane, jnp.minimum)
    m2 = _group_allreduce(jnp.where(lane_f == i1, -jnp.inf, biased), lane, jnp.maximum)
    gs = m1 + m2
    grp = (lane // per_group) % N_EXPERT_GROUPS
    beaten = jnp.zeros(biased.shape, F32)
    for k in range(1, N_EXPERT_GROUPS):
        other = pltpu.roll(gs, per_group * k, axis=1)
        other_grp = (grp - k) % N_EXPERT_GROUPS
        wins = jnp.where(other > gs, 1.0, jnp.where(other == gs, jnp.where(other_grp < grp, 1.0, 0.0), 0.0))
        beaten = beaten + wins
    keep = jnp.where(lane < N_EXPERTS, jnp.where(beaten < TOPK_GROUPS, 1.0, 0.0), 0.0)
    masked = jnp.where(keep > 0.0, biased, -jnp.inf)
    idx_out = jnp.zeros(biased.shape, F32)
    w_out = jnp.zeros(biased.shape, F32)
    w_sum = jnp.zeros((biased.shape[0], 1), F32)
    member = jnp.zeros(biased.shape, F32)
    picks = []
    for k in range(TOP_K):
        m = jnp.max(masked, axis=1, keepdims=True)
        ik = jnp.min(jnp.where(masked == m, lane_f, 1e9), axis=1, keepdims=True)
        pick = lane_f == ik
        wk = jnp.sum(jnp.where(pick, scores, 0.0), axis=1, keepdims=True)
        masked = jnp.where(pick, -jnp.inf, masked)
        member = jnp.where(pick, 1.0, member)
        picks.append(pick)
        idx_out = jnp.where(lane == k, ik, idx_out)
        w_out = jnp.where(lane == k, wk, w_out)
        w_sum = w_sum + wk
    member = jnp.where(real, member, 0.0)
    ri = lax.broadcasted_iota(I32, (rows, rows), 0)
    ci = lax.broadcasted_iota(I32, (rows, rows), 1)
    before = jnp.where(ri > ci, 1.0, 0.0).astype(BF16)
    pos = cnt_scr[...] + dotf(before, member.astype(BF16))
    rank_out = jnp.zeros(biased.shape, F32)
    for k in range(TOP_K):
        rk = jnp.sum(jnp.where(picks[k], pos, 0.0), axis=1, keepdims=True)
        rank_out = jnp.where(lane == k, rk, rank_out)

    @pl.when(real)
    def _():
        idx_ref[...] = idx_out.astype(I32)
        wt_ref[...] = w_out / w_sum * ROUTED_SCALE
        rank_ref[...] = rank_out.astype(I32)

    total = cnt_scr[...] + jnp.sum(member, axis=0, keepdims=True)
    cnt_scr[...] = total
    cnt_ref[...] = jnp.broadcast_to(total, cnt_ref.shape)


def _ffnpre(x, g, mod, wr_hi, wr_lo, rb, cnt_in, tm, hp_prev=None, hp_tok=None):
    t, d = x.shape
    r = mod.shape[0]
    n_real = t // tm
    steps = n_real if hp_prev is not None else hp_tok // tm
    row_map = lambda i: (jnp.minimum(i, n_real - 1), 0)
    const = lambda i: (0, 0)
    in_specs = [pl.BlockSpec((tm, d), row_map),
                pl.BlockSpec((1, d), const),
                pl.BlockSpec((r, d), lambda i: (0, 3)),
                pl.BlockSpec((r, d), lambda i: (0, 4)),
                pl.BlockSpec((d, LANES), const),
                pl.BlockSpec((d, LANES), const),
                pl.BlockSpec((1, LANES), const),
                pl.BlockSpec((1, LANES), const)]
    args = [x, g, mod, mod, wr_hi, wr_lo, rb, cnt_in]
    small = [pl.BlockSpec((tm, LANES), row_map), pl.BlockSpec((tm, LANES), row_map),
             pl.BlockSpec((tm, LANES), row_map), pl.BlockSpec((TILE_ROWS, LANES), const)]
    small_shape = [jax.ShapeDtypeStruct((t, LANES), I32), jax.ShapeDtypeStruct((t, LANES), F32),
                   jax.ShapeDtypeStruct((t, LANES), I32), jax.ShapeDtypeStruct((TILE_ROWS, LANES), F32)]
    inner = functools.partial(_ffnpre_body, rows=tm, n_real=n_real)
    n_in = len(args)
    if hp_prev is None:
        out_specs = [pl.BlockSpec((tm * TILE_ROWS, LANES), lambda i: (i, 0))] + small
        out_shape = [jax.ShapeDtypeStruct((hp_tok * TILE_ROWS, LANES), U32)] + small_shape
        body = inner
        aliases = {}
    else:
        start, blk = hp_tok
        in_specs.append(pl.BlockSpec(memory_space=pl.ANY))
        args.append(hp_prev)
        out_specs = [pl.BlockSpec((blk * TILE_ROWS, LANES), lambda i: (start // blk, 0))] + small
        out_shape = [jax.ShapeDtypeStruct(hp_prev.shape, U32)] + small_shape
        body = lambda *refs: inner(*refs[:n_in], *refs[n_in + 1:])
        aliases = {n_in: 0}
    return pl.pallas_call(
        body, grid=(steps,), in_specs=in_specs, out_specs=out_specs, out_shape=out_shape,
        input_output_aliases=aliases, scratch_shapes=[pltpu.VMEM((1, LANES), F32)],
        compiler_params=_cparams(("arbitrary",), 40), name="ffnpre")(*args)


def _tile_rows(r):
    return pl.ds(pl.multiple_of(r * TILE_ROWS, TILE_ROWS), TILE_ROWS)


def _route_body(dest_ref, pstart_ref, cnt_ref, hp_ref, xs_ref, ztile, sem, zsem, *, tm, n_rows):
    i = pl.program_id(0)

    def zero_rows(lo, hi):
        def zcopy(r):
            return pltpu.make_async_copy(ztile, xs_ref.at[_tile_rows(r)], zsem.at[0])

        def start(r, carry):
            zcopy(r).start()
            return carry

        def wait(r, carry):
            zcopy(r).wait()
            return carry
        lax.fori_loop(lo, hi, start, 0)
        lax.fori_loop(lo, hi, wait, 0)

    def padded_end(e):
        cnt = cnt_ref[e]
        return pstart_ref[e] + (((cnt + MOE_ROWS - 1) // MOE_ROWS) * MOE_ROWS)

    @pl.when(i == 0)
    def _():
        ztile[...] = jnp.zeros(ztile.shape, U32)

        def per_expert(e, carry):
            zero_rows(pstart_ref[e] + cnt_ref[e], padded_end(e))
            return carry
        lax.fori_loop(0, N_EXPERTS, per_expert, 0)
        zero_rows(padded_end(N_EXPERTS - 1), n_rows)

    per_tile = tm * TOP_K

    def copy(r):
        row = dest_ref[i * per_tile + r]
        return pltpu.make_async_copy(hp_ref.at[_tile_rows(r // TOP_K)], xs_ref.at[_tile_rows(row)], sem.at[0])

    def start(r, carry):
        copy(r).start()
        return carry

    def wait(r, carry):
        copy(r).wait()
        return carry
    lax.fori_loop(0, per_tile, start, 0, unroll=8)
    lax.fori_loop(0, per_tile, wait, 0, unroll=8)


def _route(dest, pstart, counts, hp, n_tok, tm, n_rows):
    assert n_tok % tm == 0
    grid_spec = pltpu.PrefetchScalarGridSpec(
        num_scalar_prefetch=3, grid=(n_tok // tm,),
        in_specs=[pl.BlockSpec((tm * TILE_ROWS, LANES), lambda i, cd, ps, cn: (i, 0))],
        out_specs=pl.BlockSpec(memory_space=pl.ANY),
        scratch_shapes=[pltpu.VMEM((TILE_ROWS, LANES), U32),
                        pltpu.SemaphoreType.DMA((1,)), pltpu.SemaphoreType.DMA((1,))])
    return pl.pallas_call(
        functools.partial(_route_body, tm=tm, n_rows=n_rows), grid_spec=grid_spec,
        out_shape=jax.ShapeDtypeStruct((n_rows * TILE_ROWS, LANES), U32),
        compiler_params=_cparams(("arbitrary",), 32), name="route")(dest, pstart, counts, hp)


def _moe_body(blk_e_ref, nlive_ref, xs_ref, wg_ref, wu_ref, wd_ref, y_ref, wg_b, wu_b, wd_b):
    b = pl.program_id(0)
    live = b < nlive_ref[0]
    e = blk_e_ref[b]
    e_prev = blk_e_ref[jnp.maximum(b - 1, 0)]

    @pl.when(jnp.logical_and(live, jnp.logical_or(b == 0, e != e_prev)))
    def _():
        wg_b[...] = wg_ref[0].astype(BF16)
        wu_b[...] = wu_ref[0].astype(BF16)
        wd_b[...] = wd_ref[0].astype(BF16)

    @pl.when(live)
    def _():
        lo, hi = _unpack_pair(_load_token_tiles(xs_ref, MOE_ROWS))
        x = jnp.concatenate([lo, hi], axis=1).astype(BF16)
        dotf = functools.partial(jnp.dot, preferred_element_type=F32)
        act = (_silu(dotf(x, wg_b[...])) * dotf(x, wu_b[...])).astype(BF16)
        y = dotf(act, wd_b[...])
        half = D_MODEL // 2
        _store_token_tiles(y_ref, _pack_pair(y[:, :half], y[:, half:]), MOE_ROWS)

    @pl.when(jnp.logical_not(live))
    def _():
        y_ref[...] = jnp.zeros(y_ref.shape, U32)


def _moe(blk_e, n_live, xs, wg, wu, wd):
    n_blk = blk_e.shape[0]
    blk_rows = MOE_ROWS * TILE_ROWS
    grid_spec = pltpu.PrefetchScalarGridSpec(
        num_scalar_prefetch=2, grid=(n_blk,),
        in_specs=[pl.BlockSpec((blk_rows, LANES), lambda b, be, nl: (b, 0)),
                  pl.BlockSpec((1, D_MODEL, D_EXPERT), lambda b, be, nl: (be[b], 0, 0)),
                  pl.BlockSpec((1, D_MODEL, D_EXPERT), lambda b, be, nl: (be[b], 0, 0)),
                  pl.BlockSpec((1, D_EXPERT, D_MODEL), lambda b, be, nl: (be[b], 0, 0))],
        out_specs=pl.BlockSpec((blk_rows, LANES), lambda b, be, nl: (b, 0)),
        scratch_shapes=[pltpu.VMEM((D_MODEL, D_EXPERT), BF16), pltpu.VMEM((D_MODEL, D_EXPERT), BF16),
                        pltpu.VMEM((D_EXPERT, D_MODEL), BF16)])
    return pl.pallas_call(
        _moe_body, grid_spec=grid_spec,
        out_shape=jax.ShapeDtypeStruct(xs.shape, U32),
        compiler_params=_cparams(("arbitrary",), 48), name="moe")(blk_e, n_live, xs, wg, wu, wd)


def _expert_layout(cnt_total, n_assign):
    assert n_assign % MOE_ROWS == 0
    counts = cnt_total[0, :N_EXPERTS].astype(I32)
    padded = (counts + MOE_ROWS - 1) // MOE_ROWS * MOE_ROWS
    pends = jnp.cumsum(padded)
    pstart = (pends - padded).astype(I32)
    n_blk = n_assign // MOE_ROWS + N_EXPERTS
    blk_start = jnp.arange(n_blk, dtype=I32) * MOE_ROWS
    blk_e = jnp.minimum(jnp.sum((pends[None, :] <= blk_start[:, None]).astype(I32), axis=1), N_EXPERTS - 1)
    n_live = (pends[-1] // MOE_ROWS).astype(I32).reshape(1)
    return blk_e.astype(I32), n_live, pstart, counts, n_blk * MOE_ROWS


def _combine_body(dest_ref, x_ref, y_ref, hp_ref, wt_ref, wgs_ref, wus_ref, wds_ref, g_ref, nf_ref,
                  o_ref, gbuf, sem, *, tm, tok0):
    i = pl.program_id(0)
    n = pl.num_programs(0)
    slot = i % 2
    per_tile = tm * TOP_K

    def copy(j, r, sl):
        row = dest_ref[(tok0 + j * tm) * TOP_K + r]
        return pltpu.make_async_copy(y_ref.at[_tile_rows(row)], gbuf.at[sl, _tile_rows(r)], sem.at[sl])

    def for_assignments(fn):
        def step(r, carry):
            fn(r)
            return carry
        lax.fori_loop(0, per_tile, step, 0, unroll=8)

    @pl.when(i == 0)
    def _():
        for_assignments(lambda r: copy(0, r, 0).start())

    @pl.when(i + 1 < n)
    def _():
        for_assignments(lambda r: copy(i + 1, r, 1 - slot).start())

    for_assignments(lambda r: copy(i, r, slot).wait())
    wts = wt_ref[...]
    acc_lo = [None] * TILE_ROWS
    acc_hi = [None] * TILE_ROWS
    stride = TOP_K * TILE_ROWS
    for k in range(TOP_K):
        wk = wts[:, k:k + 1]
        for s in range(TILE_ROWS):
            l_k, h_k = _unpack_pair(gbuf[slot, pl.ds(k * TILE_ROWS + s, tm, stride=stride), :])
            acc_lo[s] = l_k * wk if acc_lo[s] is None else acc_lo[s] + l_k * wk
            acc_hi[s] = h_k * wk if acc_hi[s] is None else acc_hi[s] + h_k * wk
    routed = jnp.concatenate(acc_lo + acc_hi, axis=1)
    hl, hh = _unpack_pair(_load_token_tiles(hp_ref, tm))
    h = jnp.concatenate([hl, hh], axis=1).astype(BF16)
    dotf = functools.partial(jnp.dot, preferred_element_type=F32)
    act = (_silu(dotf(h, wgs_ref[...])) * dotf(h, wus_ref[...])).astype(BF16)
    shared = dotf(act, wds_ref[...])
    xo = x_ref[...] + g_ref[...] * (routed + shared)
    r = lax.rsqrt(jnp.mean(xo * xo, axis=-1, keepdims=True) + RMS_EPS)
    o_ref[...] = (xo * r) * nf_ref[...]


def _combine(dest, x1, ys, hp, wts, wgs, wus, wds, mod, norm_final, tm, tok0):
    t, d = x1.shape
    r = mod.shape[0]
    o = tok0 // tm
    const = lambda i, ds: (0, 0)
    grid_spec = pltpu.PrefetchScalarGridSpec(
        num_scalar_prefetch=1, grid=(t // tm,),
        in_specs=[pl.BlockSpec((tm, d), lambda i, ds: (i, 0)),
                  pl.BlockSpec(memory_space=pl.ANY),
                  pl.BlockSpec((tm * TILE_ROWS, LANES), lambda i, ds: (i + o, 0)),
                  pl.BlockSpec((tm, LANES), lambda i, ds: (i, 0)),
                  pl.BlockSpec((d, D_EXPERT), const),
                  pl.BlockSpec((d, D_EXPERT), const),
                  pl.BlockSpec((D_EXPERT, d), const),
                  pl.BlockSpec((r, d), lambda i, ds: (0, 5)),
                  pl.BlockSpec((1, d), const)],
        out_specs=pl.BlockSpec((tm, d), lambda i, ds: (i, 0)),
        scratch_shapes=[pltpu.VMEM((2, tm * TOP_K * TILE_ROWS, LANES), U32), pltpu.SemaphoreType.DMA((2,))])
    return pl.pallas_call(
        functools.partial(_combine_body, tm=tm, tok0=tok0), grid_spec=grid_spec,
        out_shape=jax.ShapeDtypeStruct((t, d), F32),
        compiler_params=_cparams(("arbitrary",), 48), name="combine")(
            dest, x1, ys, hp, wts, wgs, wus, wds, mod, norm_final)


PAGES_PER_STEP = 16


def _pagesum_body(pt_ref, *refs):
    ins = refs[:PAGES_PER_STEP]
    o_ref = refs[PAGES_PER_STEP]
    ppb = MOBA_BLOCK // PAGE_SIZE
    for j in range(PAGES_PER_STEP // ppb):
        acc = None
        for p in range(ppb):
            s = jnp.sum(ins[j * ppb + p][0].reshape(PAGE_SIZE, N_HEADS, HEAD_DIM), axis=0)
            acc = s if acc is None else acc + s
        for h in range(N_HEADS):
            o_ref[0, j:j + 1, h * HEAD_DIM:(h + 1) * HEAD_DIM] = acc[h:h + 1, :]


def _pagesum(page_table_flat, kpool, n_batch, n_pages):
    w = N_HEADS * HEAD_DIM
    ppb = MOBA_BLOCK // PAGE_SIZE
    steps = n_pages // PAGES_PER_STEP

    def in_map(p):
        return lambda b, j, pt: (pt[b * n_pages + j * PAGES_PER_STEP + p], 0, 0)

    grid_spec = pltpu.PrefetchScalarGridSpec(
        num_scalar_prefetch=1, grid=(n_batch, steps),
        in_specs=[pl.BlockSpec((1, PAGE_SIZE * N_HEADS, HEAD_DIM), in_map(p)) for p in range(PAGES_PER_STEP)],
        out_specs=pl.BlockSpec((1, PAGES_PER_STEP // ppb, w), lambda b, j, pt: (b, j, 0)))
    return pl.pallas_call(
        _pagesum_body, grid_spec=grid_spec,
        out_shape=jax.ShapeDtypeStruct((n_batch, n_pages // ppb, w), F32),
        compiler_params=_cparams(("arbitrary", "arbitrary"), 40), name="pagesum")(
            page_table_flat, *([kpool] * PAGES_PER_STEP))


def _gate3_body(q_ref, ks_ref, sel_ref):
    prod = ks_ref[0] * q_ref[0] * (1.0 / MOBA_BLOCK)
    g = jnp.concatenate([jnp.sum(prod[:, h * HEAD_DIM:(h + 1) * HEAD_DIM], axis=1, keepdims=True)
                         for h in range(N_HEADS)], axis=1)
    rowi = lax.broadcasted_iota(I32, g.shape, 0).astype(F32)
    picks = []
    for _ in range(MOBA_TOPK):
        m = jnp.max(g, axis=0, keepdims=True)
        ik = jnp.min(jnp.where(g == m, rowi, 1e9), axis=0, keepdims=True)
        g = jnp.where(rowi == ik, -jnp.inf, g)
        picks.append(ik)
    picks.append(jnp.zeros((8 - MOBA_TOPK, N_HEADS), F32))
    sel_ref[0] = jnp.concatenate(picks, axis=0).astype(I32)


def _gate3(q3, ksum):
    n_batch, nb, w = ksum.shape
    return pl.pallas_call(
        _gate3_body, grid=(n_batch,),
        in_specs=[pl.BlockSpec((1, 1, w), lambda b: (b, 0, 0)),
                  pl.BlockSpec((1, nb, w), lambda b: (b, 0, 0))],
        out_specs=pl.BlockSpec((1, 8, N_HEADS), lambda b: (b, 0, 0)),
        out_shape=jax.ShapeDtypeStruct((n_batch, 8, N_HEADS), I32),
        compiler_params=_cparams(("arbitrary",), 32), name="gate3")(q3, ksum)


def _decode_body(phys_ref, blk_ref, q_ref, kn_ref, vn_ref, *refs, last_block):
    ppb = MOBA_BLOCK // PAGE_SIZE
    n_pg = MOBA_TOPK * ppb
    k_refs = refs[:n_pg]
    v_refs = refs[n_pg:2 * n_pg]
    tab_ref = refs[2 * n_pg]
    o_ref = refs[2 * n_pg + 1]
    b = pl.program_id(0)
    h = pl.program_id(1)
    q = q_ref[0, 0]
    scale = HEAD_DIM ** -0.5
    qb = jnp.broadcast_to(q, (8, HEAD_DIM)).astype(BF16)
    dg = functools.partial(lax.dot_general, dimension_numbers=NT_DIMS, preferred_element_type=F32)
    s_self = jnp.sum(q * kn_ref[0, 0], axis=1, keepdims=True) * scale + tab_ref[0, 2:3, 0:1]
    scores = []
    vals = []
    head_rows = pl.ds(h, PAGE_SIZE, stride=N_HEADS)
    for s in range(MOBA_TOPK):
        kblk = jnp.concatenate([k_refs[s * ppb + p][0, head_rows, :] for p in range(ppb)], axis=0).astype(BF16)
        vals.append(jnp.concatenate([v_refs[s * ppb + p][0, head_rows, :] for p in range(ppb)], axis=0).astype(BF16))
        n = blk_ref[(b * N_HEADS + h) * MOBA_TOPK + s]
        bias = jnp.where(n == last_block, tab_ref[0, 0:1, :], tab_ref[0, 1:2, :])
        scores.append(dg(qb, kblk)[0:1, :] * scale + bias)
    m = s_self
    for sc in scores:
        m = jnp.maximum(m, jnp.max(sc, axis=1, keepdims=True))
    p_self = jnp.exp(s_self - m)
    l = p_self
    acc = p_self * vn_ref[0, 0]
    for sc, vblk in zip(scores, vals):
        p = jnp.exp(sc - m)
        l = l + jnp.sum(p, axis=1, keepdims=True)
        pb = jnp.broadcast_to(p, (8, MOBA_BLOCK)).astype(BF16)
        acc = acc + jnp.dot(pb, vblk, preferred_element_type=F32)[0:1, :]
    o_ref[0, 0] = acc * (1.0 / l)


def _decode(phys, blks, q4, kn4, vn4, kpool, vpool, tabs, last_block):
    n_batch = q4.shape[0]
    ppb = MOBA_BLOCK // PAGE_SIZE
    n_pg = MOBA_TOPK * ppb
    hd = HEAD_DIM

    def pg_map(i):
        return lambda b, h, ph, bl: (ph[(b * N_HEADS + h) * n_pg + i], 0, 0)

    vec = pl.BlockSpec((1, 1, 1, hd), lambda b, h, ph, bl: (b, h, 0, 0))
    page = (1, PAGE_SIZE * N_HEADS, hd)
    grid_spec = pltpu.PrefetchScalarGridSpec(
        num_scalar_prefetch=2, grid=(n_batch, N_HEADS),
        in_specs=[vec, vec, vec]
        + [pl.BlockSpec(page, pg_map(i)) for i in range(n_pg)]
        + [pl.BlockSpec(page, pg_map(i)) for i in range(n_pg)]
        + [pl.BlockSpec((1, 8, MOBA_BLOCK), lambda b, h, ph, bl: (h, 0, 0))],
        out_specs=vec)
    return pl.pallas_call(
        functools.partial(_decode_body, last_block=last_block), grid_spec=grid_spec,
        out_shape=jax.ShapeDtypeStruct((n_batch, N_HEADS, 1, hd), F32),
        compiler_params=_cparams(("arbitrary", "arbitrary"), 32), name="decode")(
            phys, blks, q4, kn4, vn4, *([kpool] * n_pg), *([vpool] * n_pg), tabs)


def _sconv_body(p_ref, cs_ref, cw_ref, cb_ref, dtb_ref, alog_ref, x_ref, b_ref, c_ref, dt_ref, da_ref, nc_ref):
    xbc = p_ref[:, COL_X:COL_X + CONV_DIM]
    cs = cs_ref[...]
    conv = cb_ref[...] + xbc * cw_ref[CONV_WIDTH - 1:CONV_WIDTH, :]
    for i in range(CONV_WIDTH - 1):
        conv = conv + cs[:, i * CONV_DIM:(i + 1) * CONV_DIM] * cw_ref[i:i + 1, :]
    nc_ref[:, 0:(CONV_WIDTH - 2) * CONV_DIM] = cs[:, CONV_DIM:]
    nc_ref[:, (CONV_WIDTH - 2) * CONV_DIM:] = xbc
    act = _silu(conv)
    gn = N_SSD_GROUPS * SSD_STATE
    x_ref[...] = act[:, :SSD_WIDTH]
    b_ref[...] = act[:, SSD_WIDTH:SSD_WIDTH + gn]
    c_ref[...] = act[:, SSD_WIDTH + gn:]
    dtv = p_ref[:, COL_DT:COL_DT + LANES] + dtb_ref[...]
    dt = jnp.maximum(dtv, 0.0) + jnp.log1p(jnp.exp(-jnp.abs(dtv)))
    dt_ref[...] = dt
    da_ref[...] = jnp.exp(dt * (-jnp.exp(alog_ref[...])))


def _sconv(proj_s, conv_state2, conv_w, conv_b, dt_bias_p, a_log_p):
    n = proj_s.shape[0]
    gn = N_SSD_GROUPS * SSD_STATE
    shapes = [(n, SSD_WIDTH), (n, gn), (n, gn), (n, LANES), (n, LANES), (n, (CONV_WIDTH - 1) * CONV_DIM)]
    return pl.pallas_call(
        _sconv_body,
        out_shape=[jax.ShapeDtypeStruct(s, F32) for s in shapes],
        compiler_params=pltpu.CompilerParams(vmem_limit_bytes=32 * 1024 * 1024), name="sconv")(
            proj_s, conv_state2, conv_w, conv_b, dt_bias_p, a_log_p)


def _sstate_body(st_ref, x_ref, dt_ref, da_ref, b_ref, c_ref, ns_ref, y_ref):
    hpg = N_SSD_HEADS // N_SSD_GROUPS
    for g in range(N_SSD_GROUPS):
        hs = slice(g * hpg, (g + 1) * hpg)
        st = st_ref[0, hs]
        xdt = x_ref[0, hs] * dt_ref[0, hs]
        ns = st * da_ref[0, hs] + xdt * b_ref[0, g]
        ns_ref[0, hs] = ns
        y_ref[0, hs] = jnp.sum(ns * c_ref[0, g], axis=-1, keepdims=True)


def _sstate(state, x4, dt4, da4, b4, c4):
    n = state.shape[0]
    h, p, s = N_SSD_HEADS, SSD_HEAD_DIM, SSD_STATE
    g = N_SSD_GROUPS
    m4 = lambda b: (b, 0, 0, 0)
    return pl.pallas_call(
        _sstate_body, grid=(n,),
        in_specs=[pl.BlockSpec((1, h, p, s), m4), pl.BlockSpec((1, h, p, 1), m4),
                  pl.BlockSpec((1, h, 1, 1), m4), pl.BlockSpec((1, h, 1, 1), m4),
                  pl.BlockSpec((1, g, 1, s), m4), pl.BlockSpec((1, g, 1, s), m4)],
        out_specs=[pl.BlockSpec((1, h, p, s), m4), pl.BlockSpec((1, h, p, 1), m4)],
        out_shape=[jax.ShapeDtypeStruct((n, h, p, s), F32), jax.ShapeDtypeStruct((n, h, p, 1), F32)],
        compiler_params=_cparams(("arbitrary",), 32), name="sstate")(state, x4, dt4, da4, b4, c4)


def _sgate_body(y_ref, x_ref, p_ref, dsk_ref, nrm_ref, o_ref):
    y = y_ref[...] + dsk_ref[...] * x_ref[...]
    y = y * _silu(p_ref[:, COL_Z:COL_Z + SSD_WIDTH])
    r = lax.rsqrt(jnp.mean(y * y, axis=-1, keepdims=True) + RMS_EPS)
    o_ref[...] = ((y * r) * nrm_ref[...]).astype(o_ref.dtype)


def _sgate(y, xs, proj_s, dskip_w, ssd_norm):
    return pl.pallas_call(
        _sgate_body, out_shape=jax.ShapeDtypeStruct(y.shape, BF16),
        compiler_params=pltpu.CompilerParams(vmem_limit_bytes=32 * 1024 * 1024), name="sgate")(
            y, xs, proj_s, dskip_w, ssd_norm)


def kernel(x_prompt, x_sample, cache_k, cache_v, state_ssm, state_conv, page_table, c_prompt, c_sample,
           w_ada, b_ada, norm_mix, norm_ffn, w_in, conv_w, conv_b, dt_bias, a_log, d_skip, ssd_norm,
           w_out, rel_bias, w_router, router_bias, w_gate_e, w_up_e, w_down_e, w_gate_s, w_up_s, w_down_s,
           norm_final):
    depth = w_ada.shape[0]
    assert depth == 1
    bp, sp, d = x_prompt.shape
    bs, sn, _ = x_sample.shape
    assert bp == 1 and sn == 1 and d == D_MODEL
    n_pages = page_table.shape[1]
    past_len = n_pages * PAGE_SIZE
    nb_prompt = sp // MOBA_BLOCK
    nb_past = past_len // MOBA_BLOCK
    l = 0
    layer = lambda a: a.reshape(a.shape[1:])

    w_in_b = jnp.pad(layer(w_in).astype(BF16), ((0, 0), (0, IN_COLS_PAD - IN_COLS)))
    w_out_b = layer(w_out).astype(BF16)
    wgs_b, wus_b, wds_b = (layer(w).astype(BF16) for w in (w_gate_s, w_up_s, w_down_s))
    wr2 = jnp.concatenate([layer(w_router), layer(w_router)], axis=1)
    wr_hi = wr2.astype(BF16)
    wr_lo = (wr2 - wr_hi.astype(F32)).astype(BF16)
    rb2 = jnp.concatenate([router_bias[l], router_bias[l]])[None, :].astype(F32)
    g_mix = norm_mix[l][None, :]
    g_ffn = norm_ffn[l][None, :]
    g_fin = norm_final[None, :]
    pad_h = LANES - N_SSD_HEADS
    dt_bias_p = jnp.pad(dt_bias[l].astype(F32), (0, pad_h))[None, :]
    a_log_p = jnp.pad(a_log[l].astype(F32), (0, pad_h))[None, :]
    dskip_w = jnp.repeat(d_skip[l].astype(F32), SSD_HEAD_DIM)[None, :]
    nrm_ssd = ssd_norm[l][None, :]
    cw = conv_w[l]
    cbias = conv_b[l][None, :]
    expand = _head_expand()
    expand_t = expand.T
    own_bias, prev_bias = _prompt_bias_tables(rel_bias)
    rel_t = rel_bias.T

    c_all = jnp.concatenate([c_prompt, jnp.zeros((8 - bp, d), F32), c_sample], axis=0)
    mod = _adaln(c_all, layer(w_ada), b_ada[l][None, :])
    mod_p = mod[0:1]
    mod_s = mod[8:8 + bs]

    xp = x_prompt.reshape(sp, d)
    proj = _inproj(xp, g_mix, mod_p, w_in_b, 1024)
    k_p, v_p, kaug, v_t, kmean = _kvpost(proj)
    kmean_pad = jnp.pad(kmean.reshape(nb_prompt, ATT_WIDTH), ((0, LANES - nb_prompt), (0, 0)))
    att = _moba_prompt(proj, kaug, v_t, kmean_pad, own_bias, prev_bias)
    ssd, ssm_p, conv_tail = _ssd_prompt(proj, cw, cbias, dt_bias_p, a_log_p, dskip_w, nrm_ssd, expand, expand_t)
    xp1 = _outproj(att, ssd, w_out_b, xp, mod_p, 1024)

    xs = x_sample.reshape(bs, d)
    proj_s = _inproj(xs, g_mix, mod_s, w_in_b, bs)
    q_s = proj_s[:, COL_Q:COL_Q + ATT_WIDTH]
    k_s = proj_s[:, COL_K:COL_K + ATT_WIDTH]
    v_s = proj_s[:, COL_V:COL_V + ATT_WIDTH]
    kpool = cache_k.reshape(cache_k.shape[1], PAGE_SIZE * N_HEADS, HEAD_DIM)
    vpool = cache_v.reshape(cache_v.shape[1], PAGE_SIZE * N_HEADS, HEAD_DIM)
    ksum = _pagesum(page_table.reshape(-1), kpool, bs, n_pages)
    sel = _gate3(q_s.reshape(bs, 1, ATT_WIDTH), ksum)
    blks = jnp.transpose(sel[:, :MOBA_TOPK, :], (0, 2, 1))
    ppb = MOBA_BLOCK // PAGE_SIZE
    lpage = blks[..., None] * ppb + jnp.arange(ppb, dtype=I32)
    phys = jnp.take_along_axis(page_table, lpage.reshape(bs, -1), axis=1)
    dist_last = past_len - ((nb_past - 1) * MOBA_BLOCK + jnp.arange(MOBA_BLOCK))
    tab_last = _bias_lookup(rel_t, _rel_bucket(dist_last))
    tab_far = jnp.broadcast_to(rel_t[:, NUM_BUCKETS - 1:NUM_BUCKETS], tab_last.shape)
    tab_self = jnp.broadcast_to(rel_t[:, 0:1], tab_last.shape)
    tabs = jnp.stack([tab_last, tab_far, tab_self] + [jnp.zeros_like(tab_last)] * 5, axis=1).astype(F32)
    shape4 = (bs, N_HEADS, 1, HEAD_DIM)
    att_s = _decode(phys.reshape(-1).astype(I32), blks.reshape(-1).astype(I32), q_s.reshape(shape4),
                    k_s.reshape(shape4), v_s.reshape(shape4), kpool, vpool, tabs, nb_past - 1)
    att_s = att_s.reshape(bs, ATT_WIDTH).astype(BF16)

    xs_act, b_s, c_s, dt_s, da_s, conv_s = _sconv(
        proj_s, state_conv.reshape(bs, (CONV_WIDTH - 1) * CONV_DIM), cw, cbias, dt_bias_p, a_log_p)
    ssm_s, y_s = _sstate(
        layer(state_ssm), xs_act.reshape(bs, N_SSD_HEADS, SSD_HEAD_DIM, 1),
        dt_s[:, :N_SSD_HEADS].reshape(bs, N_SSD_HEADS, 1, 1), da_s[:, :N_SSD_HEADS].reshape(bs, N_SSD_HEADS, 1, 1),
        b_s.reshape(bs, N_SSD_GROUPS, 1, SSD_STATE), c_s.reshape(bs, N_SSD_GROUPS, 1, SSD_STATE))
    ssd_s = _sgate(y_s.reshape(bs, SSD_WIDTH), xs_act, proj_s, dskip_w, nrm_ssd)
    xs1 = _outproj(att_s, ssd_s, w_out_b, xs, mod_s, bs)

    n_tok = sp + bs
    n_assign = n_tok * TOP_K
    ffn_tm = 256
    assert sp % ffn_tm == 0 and sp % bs == 0 and bs <= ffn_tm
    cnt0 = jnp.zeros((1, LANES), F32)
    hp, idx_p, wt_p, rank_p, cnt_p = _ffnpre(xp1, g_ffn, mod_p, wr_hi, wr_lo, rb2, cnt0, ffn_tm,
                                             hp_tok=sp + ffn_tm)
    hp, idx_s, wt_s, rank_s, cnt_all = _ffnpre(xs1, g_ffn, mod_s, wr_hi, wr_lo, rb2, cnt_p[0:1], bs,
                                               hp_prev=hp, hp_tok=(sp, bs))
    idx = jnp.concatenate([idx_p[:, :TOP_K], idx_s[:, :TOP_K]], axis=0)
    rank = jnp.concatenate([rank_p[:, :TOP_K], rank_s[:, :TOP_K]], axis=0)
    blk_e, n_live, pstart, counts, n_rows = _expert_layout(cnt_all, n_assign)
    experts = jnp.arange(N_EXPERTS, dtype=I32)
    dest = (rank + jnp.sum(jnp.where(idx[..., None] == experts, pstart, 0), axis=-1)).reshape(-1).astype(I32)
    x_sorted = _route(dest, pstart, counts, hp, n_tok, bs, n_rows)
    y_sorted = _moe(blk_e, n_live, x_sorted, layer(w_gate_e), layer(w_up_e), layer(w_down_e))
    y_p = _combine(dest, xp1, y_sorted, hp, wt_p, wgs_b, wus_b, wds_b, mod_p, g_fin, 128, 0)
    y_s_out = _combine(dest, xs1, y_sorted, hp, wt_s, wgs_b, wus_b, wds_b, mod_s, g_fin, bs, sp)

    kv_shape = (1, bp, sp, N_HEADS, HEAD_DIM)
    kvs_shape = (1, bs, sn, N_HEADS, HEAD_DIM)
    return (y_p.reshape(bp, sp, d), y_s_out.reshape(bs, sn, d),
            k_p.reshape(kv_shape), v_p.reshape(kv_shape),
            k_s.reshape(kvs_shape), v_s.reshape(kvs_shape),
            ssm_p.reshape(1, bp, N_SSD_HEADS, SSD_HEAD_DIM, SSD_STATE),
            ssm_s.reshape(1, bs, N_SSD_HEADS, SSD_HEAD_DIM, SSD_STATE),
            conv_tail[8 - (CONV_WIDTH - 1):].reshape(1, bp, CONV_WIDTH - 1, CONV_DIM),
            conv_s.reshape(1, bs, CONV_WIDTH - 1, CONV_DIM))
```

```python
import functools
import math

import jax
import jax.numpy as jnp
from jax import lax
from jax.experimental import pallas as pl
from jax.experimental.pallas import tpu as pltpu

F32 = jnp.float32
BF16 = jnp.bfloat16
I32 = jnp.int32
U32 = jnp.uint32

D_MODEL = 2048
ATT_WIDTH = 1024
HEAD_DIM = 128
N_HEADS = 8
MOBA_BLOCK = 256
MOBA_TOPK = 3
Q_BLOCK = 128
NUM_BUCKETS = 32
MAX_DISTANCE = 128
SSD_WIDTH = 1024
SSD_HEAD_DIM = 64
N_SSD_HEADS = 16
N_SSD_GROUPS = 2
SSD_STATE = 128
CONV_WIDTH = 4
SSD_CHUNK = 256
CONV_DIM = SSD_WIDTH + 2 * N_SSD_GROUPS * SSD_STATE
N_EXPERTS = 64
TOP_K = 8
N_EXPERT_GROUPS = 8
TOPK_GROUPS = 4
D_EXPERT = 512
ROUTED_SCALE = 2.5
RMS_EPS = 1e-6
PAGE_SIZE = 128

LANES = 128
TILE_ROWS = 8
LOG2E = 1.4426950408889634
MOBA_GROUP = 8
MOBA_SUB = 2
COL_Q, COL_K, COL_V, COL_Z, COL_X = 0, 1024, 2048, 3072, 4096
COL_B, COL_C, COL_DT = 5120, 5376, 5632
IN_COLS = 5648
IN_COLS_PAD = 6144
PROJ_TN = 512
MOE_ROWS = 256
MASK_BIG = 2.0 ** 40
NT_DIMS = (((1,), (1,)), ((), ()))


def _cparams(sem, vmem_mb):
    return pltpu.CompilerParams(dimension_semantics=sem, vmem_limit_bytes=vmem_mb * 1024 * 1024)


def _silu(x):
    return x * jax.nn.sigmoid(x)


def _norm_mod(x, g, shift, scale):
    r = lax.rsqrt(jnp.mean(x * x, axis=-1, keepdims=True) + RMS_EPS)
    return (x * r) * g * (1.0 + scale) + shift


def _split2(x):
    hi = x.astype(BF16)
    lo = (x - hi.astype(F32)).astype(BF16)
    return hi, lo


def _split3(x):
    h1 = x.astype(BF16)
    r1 = x - h1.astype(F32)
    h2 = r1.astype(BF16)
    h3 = (r1 - h2.astype(F32)).astype(BF16)
    return h1, h2, h3


def _pack_pair(lo, hi):
    lo_b = lax.bitcast_convert_type(lo.astype(BF16).astype(F32), U32) >> 16
    hi_b = lax.bitcast_convert_type(hi.astype(BF16).astype(F32), U32) & jnp.uint32(0xFFFF0000)
    return lo_b | hi_b


def _unpack_pair(u):
    lo = lax.bitcast_convert_type(u << 16, F32)
    hi = lax.bitcast_convert_type(u & jnp.uint32(0xFFFF0000), F32)
    return lo, hi


def _adaln_body(c_ref, w_ref, b_ref, o_ref):
    s = _silu(c_ref[...])
    o_ref[...] = jnp.dot(s.astype(BF16), w_ref[...].astype(BF16), preferred_element_type=F32) + b_ref[...]


def _adaln(c_all, w, b):
    r, d = c_all.shape
    n = w.shape[1]
    tn = 1024
    return pl.pallas_call(
        _adaln_body, grid=(n // tn,),
        in_specs=[pl.BlockSpec((r, d), lambda j: (0, 0)),
                  pl.BlockSpec((d, tn), lambda j: (0, j)),
                  pl.BlockSpec((1, tn), lambda j: (0, j))],
        out_specs=pl.BlockSpec((r, tn), lambda j: (0, j)),
        out_shape=jax.ShapeDtypeStruct((r, n), F32),
        compiler_params=_cparams(("arbitrary",), 40), name="adaln")(c_all, w, b)


def _inproj_body(x_ref, g_ref, sh_ref, sc_ref, w_ref, o_ref, h_scr):
    @pl.when(pl.program_id(1) == 0)
    def _():
        h_scr[...] = _norm_mod(x_ref[...], g_ref[...], sh_ref[...], sc_ref[...]).astype(BF16)

    o_ref[...] = jnp.dot(h_scr[...], w_ref[...], preferred_element_type=F32)


def _inproj(x, g, mod, w_pad, tm):
    t, d = x.shape
    r = mod.shape[0]
    n = w_pad.shape[1]
    return pl.pallas_call(
        _inproj_body, grid=(t // tm, n // PROJ_TN),
        in_specs=[pl.BlockSpec((tm, d), lambda i, j: (i, 0)),
                  pl.BlockSpec((1, d), lambda i, j: (0, 0)),
                  pl.BlockSpec((r, d), lambda i, j: (0, 0)),
                  pl.BlockSpec((r, d), lambda i, j: (0, 1)),
                  pl.BlockSpec((d, PROJ_TN), lambda i, j: (0, j))],
        out_specs=pl.BlockSpec((tm, PROJ_TN), lambda i, j: (i, j)),
        out_shape=jax.ShapeDtypeStruct((t, n), F32),
        scratch_shapes=[pltpu.VMEM((tm, d), BF16)],
        compiler_params=_cparams(("arbitrary", "arbitrary"), 48), name="inproj")(x, g, mod, mod, w_pad)


def _kvpost_body(k_ref, v_ref, ko, vo, kao, vto, kmo):
    k = k_ref[...]
    v = v_ref[...]
    ko[...] = k
    vo[...] = v
    rows = k.shape[0]
    row = lax.broadcasted_iota(I32, (rows, LANES), 0)
    lane = lax.broadcasted_iota(I32, (rows, LANES), 1)
    blk = pl.program_id(0) * (rows // MOBA_BLOCK) + row // MOBA_BLOCK
    onehot = jnp.where(lane == blk, 1.0, 0.0).astype(BF16)
    kb = k.astype(BF16)
    pieces = []
    for h in range(N_HEADS):
        pieces += [kb[:, h * HEAD_DIM:(h + 1) * HEAD_DIM], onehot]
    kao[...] = jnp.concatenate(pieces, axis=1)
    vto[...] = v.T.astype(BF16)
    for i in range(rows // MOBA_BLOCK):
        kmo[i] = jnp.sum(k[i * MOBA_BLOCK:(i + 1) * MOBA_BLOCK], axis=0, keepdims=True) * (1.0 / MOBA_BLOCK)


def _kvpost(proj):
    t = proj.shape[0]
    r = 512
    w = ATT_WIDTH
    nb = t // MOBA_BLOCK
    return pl.pallas_call(
        _kvpost_body, grid=(t // r,),
        in_specs=[pl.BlockSpec((r, w), lambda i: (i, COL_K // w)),
                  pl.BlockSpec((r, w), lambda i: (i, COL_V // w))],
        out_specs=[pl.BlockSpec((r, w), lambda i: (i, 0)),
                   pl.BlockSpec((r, w), lambda i: (i, 0)),
                   pl.BlockSpec((r, 2 * w), lambda i: (i, 0)),
                   pl.BlockSpec((w, r), lambda i: (0, i)),
                   pl.BlockSpec((r // MOBA_BLOCK, 1, w), lambda i: (i, 0, 0))],
        out_shape=[jax.ShapeDtypeStruct((t, w), F32), jax.ShapeDtypeStruct((t, w), F32),
                   jax.ShapeDtypeStruct((t, 2 * w), BF16), jax.ShapeDtypeStruct((w, t), BF16),
                   jax.ShapeDtypeStruct((nb, 1, w), F32)],
        compiler_params=_cparams(("arbitrary",), 40), name="kvpost")(proj, proj)


def _rel_bucket(dist):
    d = jnp.maximum(dist, 0)
    max_exact = NUM_BUCKETS // 2
    ratio = jnp.log(jnp.maximum(d, 1).astype(F32) / max_exact) / math.log(MAX_DISTANCE / max_exact)
    large = jnp.minimum(max_exact + (ratio * (NUM_BUCKETS - max_exact)).astype(I32), NUM_BUCKETS - 1)
    return jnp.where(d < max_exact, d, large)


def _bias_lookup(rel_t, bucket):
    out = jnp.zeros((rel_t.shape[0],) + bucket.shape, F32)
    for b in range(NUM_BUCKETS):
        out = jnp.where(bucket[None] == b, rel_t[:, b].reshape((-1,) + (1,) * bucket.ndim), out)
    return out


def _prompt_bias_tables(rel_bias):
    rel_t = rel_bias.T
    far = rel_t[:, NUM_BUCKETS - 1][:, None, None]
    j = jnp.arange(MOBA_BLOCK)[:, None]
    i = jnp.arange(MOBA_BLOCK)[None, :]
    d_own = i - j
    own = jnp.where(d_own >= 0, (_bias_lookup(rel_t, _rel_bucket(d_own)) - far) * LOG2E, -MASK_BIG)
    prev = (_bias_lookup(rel_t, _rel_bucket(d_own + MOBA_BLOCK)) - far) * LOG2E
    return own.astype(F32), prev.astype(F32)


def _moba_body(q_ref, ka_ref, vt_ref, km_ref, ob_ref, pb_ref, o_ref):
    own = pl.program_id(1)
    nq = MOBA_BLOCK
    hd = HEAD_DIM
    dotf = functools.partial(jnp.dot, preferred_element_type=F32)
    q_t = q_ref[...].T
    q_hi, q_lo = _split2(q_t)
    km_hi, km_lo = _split2(km_ref[...])
    gate = dotf(km_hi, q_hi) + dotf(km_hi, q_lo) + dotf(km_lo, q_hi)
    rowi = lax.broadcasted_iota(I32, gate.shape, 0)
    row_f = rowi.astype(F32)
    g = jnp.where(rowi < own, gate, -jnp.inf)
    sel = jnp.zeros(gate.shape, F32)
    for _ in range(MOBA_TOPK):
        m = jnp.max(g, axis=0, keepdims=True)
        idx = jnp.min(jnp.where(g == m, row_f, 1e9), axis=0, keepdims=True)
        pick = row_f == idx
        sel = jnp.where(pick, 1.0, sel)
        g = jnp.where(pick, -jnp.inf, g)
    sel = jnp.where(rowi < own, sel, 0.0)
    mq_t = jnp.where(jnp.where(rowi < own - 1, sel, 0.0) > 0.0, 0.0, -MASK_BIG).astype(BF16)
    qs_t = (q_t * (hd ** -0.5 * LOG2E)).astype(BF16)
    q_aug_t = jnp.concatenate([qs_t, mq_t], axis=0)

    prev = jnp.maximum(own - 1, 0)
    st_own = pl.multiple_of(own * MOBA_BLOCK, MOBA_BLOCK)
    st_prev = pl.multiple_of(prev * MOBA_BLOCK, MOBA_BLOCK)
    s_own = dotf(ka_ref[pl.ds(st_own, MOBA_BLOCK), 0:hd], qs_t) + ob_ref[0]
    sel_prev = jnp.max(jnp.where(rowi == own - 1, sel, 0.0), axis=0, keepdims=True)
    s_prev = dotf(ka_ref[pl.ds(st_prev, MOBA_BLOCK), 0:hd], qs_t) + pb_ref[0] \
        + jnp.where(sel_prev > 0.0, 0.0, -MASK_BIG)
    m0 = jnp.maximum(jnp.max(s_own, axis=0, keepdims=True), jnp.max(s_prev, axis=0, keepdims=True))
    p_own = jnp.exp2(s_own - m0)
    p_prev = jnp.exp2(s_prev - m0)
    l0 = jnp.sum(p_own, axis=0, keepdims=True) + jnp.sum(p_prev, axis=0, keepdims=True)
    acc0 = dotf(vt_ref[:, pl.ds(st_own, MOBA_BLOCK)], p_own.astype(BF16)) \
        + dotf(vt_ref[:, pl.ds(st_prev, MOBA_BLOCK)], p_prev.astype(BF16))

    span = MOBA_GROUP * MOBA_BLOCK

    sub = MOBA_SUB * MOBA_BLOCK

    def group(it, carry):
        m, l, acc = carry
        for j in range(MOBA_GROUP // MOBA_SUB):
            st = pl.multiple_of(it * span + j * sub, sub)
            s = dotf(ka_ref[pl.ds(st, sub), :], q_aug_t)
            m_new = jnp.maximum(m, jnp.max(s, axis=0, keepdims=True))
            p = jnp.exp2(s - m_new)
            alpha = jnp.exp2(m - m_new)
            l = alpha * l + jnp.sum(p, axis=0, keepdims=True)
            acc = alpha * acc + dotf(vt_ref[:, pl.ds(st, sub)], p.astype(BF16))
            m = m_new
        return m, l, acc

    n_groups = (jnp.maximum(own - 1, 0) + MOBA_GROUP - 1) // MOBA_GROUP
    _, l, acc = lax.fori_loop(0, n_groups, group, (m0, l0, acc0))
    o_ref[...] = (acc * (1.0 / l)).T.astype(o_ref.dtype)


def _moba_prompt(proj, kaug, v_t, kmean_pad, own_bias, prev_bias):
    t = proj.shape[0]
    hd = HEAD_DIM
    nb = t // MOBA_BLOCK
    assert nb % MOBA_GROUP == 0 and nb <= LANES
    return pl.pallas_call(
        _moba_body, grid=(N_HEADS, nb),
        in_specs=[pl.BlockSpec((MOBA_BLOCK, hd), lambda h, c: (c, h)),
                  pl.BlockSpec((t, 2 * hd), lambda h, c: (0, h)),
                  pl.BlockSpec((hd, t), lambda h, c: (h, 0)),
                  pl.BlockSpec((LANES, hd), lambda h, c: (0, h)),
                  pl.BlockSpec((1, MOBA_BLOCK, MOBA_BLOCK), lambda h, c: (h, 0, 0)),
                  pl.BlockSpec((1, MOBA_BLOCK, MOBA_BLOCK), lambda h, c: (h, 0, 0))],
        out_specs=pl.BlockSpec((MOBA_BLOCK, hd), lambda h, c: (c, h)),
        out_shape=jax.ShapeDtypeStruct((t, ATT_WIDTH), BF16),
        compiler_params=_cparams(("arbitrary", "arbitrary"), 56), name="moba_prompt")(
            proj, kaug, v_t, kmean_pad, own_bias, prev_bias)


def _ssd_body(z_ref, x_ref, b_ref, c_ref, dt_ref, cw_ref, cb_ref, dtb_ref, alog_ref, dsk_ref, nrm_ref,
              exp_ref, expt_ref, y_ref, st_ref, tail_ref, xpad, state):
    i = pl.program_id(0)
    L = SSD_CHUNK
    W = SSD_WIDTH
    GN = N_SSD_GROUPS * SSD_STATE

    @pl.when(i == 0)
    def _():
        xpad[0:8, :] = jnp.zeros((8, CONV_DIM), F32)
        state[...] = jnp.zeros(state.shape, F32)

    xpad[8:8 + L, 0:W] = x_ref[...]
    xpad[8:8 + L, W:W + GN] = b_ref[...]
    xpad[8:8 + L, W + GN:W + 2 * GN] = c_ref[...]
    conv = cb_ref[...] + xpad[5:5 + L, :] * cw_ref[0:1, :]
    for r in range(1, CONV_WIDTH):
        conv = conv + xpad[5 + r:5 + r + L, :] * cw_ref[r:r + 1, :]
    tail = xpad[L:L + 8, :]
    tail_ref[...] = tail
    xpad[0:8, :] = tail
    act = _silu(conv)
    xs = act[:, 0:W]
    bm = act[:, W:W + GN].astype(BF16)
    cm = act[:, W + GN:W + 2 * GN].astype(BF16)

    dtv = dt_ref[...] + dtb_ref[...]
    dt = jnp.maximum(dtv, 0.0) + jnp.log1p(jnp.exp(-jnp.abs(dtv)))
    a = -jnp.exp(alog_ref[...])
    adt = dt * a
    row = lax.broadcasted_iota(I32, (L, L), 0)
    col = lax.broadcasted_iota(I32, (L, L), 1)
    causal = row >= col
    tril = jnp.where(causal, 1.0, 0.0).astype(BF16)
    dotf = functools.partial(jnp.dot, preferred_element_type=F32)
    acum = sum(dotf(tril, part) for part in _split3(adt))
    acum_t = acum.T
    expand = exp_ref[...]

    def widen(v):
        return sum(dotf(part, expand) for part in _split3(v))

    xd = xs * widen(dt)
    xd_b = xd.astype(BF16)
    lane = lax.broadcasted_iota(I32, (L, LANES), 1)
    low_half = lane < SSD_HEAD_DIM
    dg = functools.partial(lax.dot_general, dimension_numbers=NT_DIMS, preferred_element_type=F32)
    hpg = N_SSD_HEADS // N_SSD_GROUPS
    y_cols = []
    cbs = [dg(cm[:, g * SSD_STATE:(g + 1) * SSD_STATE], bm[:, g * SSD_STATE:(g + 1) * SSD_STATE])
           for g in range(N_SSD_GROUPS)]
    for pair in range(N_SSD_HEADS // 2):
        cb = cbs[(2 * pair) // hpg]
        xj = xd_b[:, pair * LANES:(pair + 1) * LANES]
        acc = None
        for sub in range(2):
            h = 2 * pair + sub
            diff = acum[:, h:h + 1] - acum_t[h:h + 1, :]
            dec = jnp.exp(jnp.where(causal, diff, -jnp.inf))
            mh = (cb * dec).astype(BF16)
            xh = jnp.where(low_half if sub == 0 else jnp.logical_not(low_half), xj, jnp.zeros_like(xj))
            part = dotf(mh, xh)
            acc = part if acc is None else acc + part
        y_cols.append(acc)
    y_diag = jnp.concatenate(y_cols, axis=1)

    s_prev = state[...]
    s_b = s_prev.astype(BF16)
    half_w = W // N_SSD_GROUPS
    y_off = jnp.concatenate(
        [dg(cm[:, g * SSD_STATE:(g + 1) * SSD_STATE], s_b[g * half_w:(g + 1) * half_w, :])
         for g in range(N_SSD_GROUPS)], axis=1) * widen(jnp.exp(acum))

    dte = jnp.exp(acum[L - 1:L, :] - acum)
    xdd = xd * widen(dte)
    upd = jnp.concatenate(
        [dotf(xdd[:, g * half_w:(g + 1) * half_w].T.astype(BF16), bm[:, g * SSD_STATE:(g + 1) * SSD_STATE])
         for g in range(N_SSD_GROUPS)], axis=0)
    last = jnp.broadcast_to(acum_t[:, L - 1:L], (LANES, SSD_STATE))
    dec_rows = jnp.exp(sum(dotf(expt_ref[...], part) for part in _split3(last)))
    s_new = s_prev * dec_rows + upd
    state[...] = s_new
    st_ref[...] = s_new

    y = y_diag + y_off + dsk_ref[...] * xs
    y = y * _silu(z_ref[...])
    r = lax.rsqrt(jnp.mean(y * y, axis=-1, keepdims=True) + RMS_EPS)
    y_ref[...] = ((y * r) * nrm_ref[...]).astype(y_ref.dtype)


def _head_expand():
    h = jnp.arange(LANES)[:, None]
    ch = jnp.arange(SSD_WIDTH)[None, :]
    return jnp.where(ch // SSD_HEAD_DIM == h, 1.0, 0.0).astype(BF16)


def _ssd_prompt(proj, conv_w, conv_b, dt_bias_p, a_log_p, dskip_w, ssd_norm, expand, expand_t):
    t = proj.shape[0]
    L = SSD_CHUNK
    W = SSD_WIDTH
    GN = N_SSD_GROUPS * SSD_STATE
    const = lambda shape: pl.BlockSpec(shape, lambda i: (0, 0))
    return pl.pallas_call(
        _ssd_body, grid=(t // L,),
        in_specs=[pl.BlockSpec((L, W), lambda i: (i, COL_Z // W)),
                  pl.BlockSpec((L, W), lambda i: (i, COL_X // W)),
                  pl.BlockSpec((L, GN), lambda i: (i, COL_B // GN)),
                  pl.BlockSpec((L, GN), lambda i: (i, COL_C // GN)),
                  pl.BlockSpec((L, LANES), lambda i: (i, COL_DT // LANES)),
                  const((CONV_WIDTH, CONV_DIM)), const((1, CONV_DIM)), const((1, LANES)), const((1, LANES)),
                  const((1, W)), const((1, W)), const((LANES, W)), const((W, LANES))],
        out_specs=[pl.BlockSpec((L, W), lambda i: (i, 0)),
                   pl.BlockSpec((W, SSD_STATE), lambda i: (0, 0)),
                   pl.BlockSpec((8, CONV_DIM), lambda i: (0, 0))],
        out_shape=[jax.ShapeDtypeStruct((t, W), BF16),
                   jax.ShapeDtypeStruct((W, SSD_STATE), F32),
                   jax.ShapeDtypeStruct((8, CONV_DIM), F32)],
        scratch_shapes=[pltpu.VMEM((L + 8, CONV_DIM), F32), pltpu.VMEM((W, SSD_STATE), F32)],
        compiler_params=_cparams(("arbitrary",), 40), name="ssd_prompt")(
            proj, proj, proj, proj, proj, conv_w, conv_b, dt_bias_p, a_log_p, dskip_w, ssd_norm, expand, expand_t)


def _outproj_body(a_ref, s_ref, w_ref, x_ref, g_ref, o_ref):
    w = ATT_WIDTH
    acc = jnp.dot(a_ref[...], w_ref[0:w, :], preferred_element_type=F32)
    acc = acc + jnp.dot(s_ref[...], w_ref[w:, :], preferred_element_type=F32)
    o_ref[...] = x_ref[...] + g_ref[...] * acc


def _outproj(att, ssd, w_out_b, x, mod, tm):
    t, d = x.shape
    r = mod.shape[0]
    tn = PROJ_TN
    return pl.pallas_call(
        _outproj_body, grid=(t // tm, d // tn),
        in_specs=[pl.BlockSpec((tm, ATT_WIDTH), lambda i, j: (i, 0)),
                  pl.BlockSpec((tm, SSD_WIDTH), lambda i, j: (i, 0)),
                  pl.BlockSpec((ATT_WIDTH + SSD_WIDTH, tn), lambda i, j: (0, j)),
                  pl.BlockSpec((tm, tn), lambda i, j: (i, j)),
                  pl.BlockSpec((r, tn), lambda i, j: (0, 2 * (d // tn) + j))],
        out_specs=pl.BlockSpec((tm, tn), lambda i, j: (i, j)),
        out_shape=jax.ShapeDtypeStruct((t, d), F32),
        compiler_params=_cparams(("arbitrary", "arbitrary"), 40), name="outproj")(att, ssd, w_out_b, x, mod)


def _group_allreduce(v, lane, op):
    for s in (1, 2, 4):
        up = pltpu.roll(v, LANES - s, axis=1)
        dn = pltpu.roll(v, s, axis=1)
        v = op(v, jnp.where((lane & s) == 0, up, dn))
    return v


def _store_token_tiles(ref, packed, n_tok):
    for s in range(TILE_ROWS):
        ref[pl.ds(s, n_tok, stride=TILE_ROWS), :] = packed[:, s * LANES:(s + 1) * LANES]


def _load_token_tiles(ref, n_tok, start=0, stride=TILE_ROWS):
    return jnp.concatenate([ref[pl.ds(start + s, n_tok, stride=stride), :] for s in range(TILE_ROWS)], axis=1)


def _ffnpre_body(x_ref, g_ref, sh_ref, sc_ref, wrh_ref, wrl_ref, rb_ref, cin_ref,
                 hp_ref, idx_ref, wt_ref, rank_ref, cnt_ref, cnt_scr, *, rows, n_real):
    i = pl.program_id(0)
    real = i < n_real

    @pl.when(i == 0)
    def _():
        cnt_scr[...] = cin_ref[...]

    h = _norm_mod(x_ref[...], g_ref[...], sh_ref[...], sc_ref[...])
    half = D_MODEL // 2
    packed = _pack_pair(h[:, :half], h[:, half:])
    _store_token_tiles(hp_ref, jnp.where(real, packed, jnp.zeros_like(packed)), rows)
    if rows * TILE_ROWS != hp_ref.shape[0]:
        hp_ref[rows * TILE_ROWS:, :] = jnp.zeros((hp_ref.shape[0] - rows * TILE_ROWS, LANES), U32)
    h_hi, h_lo = _split2(h)
    dotf = functools.partial(jnp.dot, preferred_element_type=F32)
    logits = dotf(h_hi, wrh_ref[...]) + dotf(h_hi, wrl_ref[...]) + dotf(h_lo, wrh_ref[...])
    scores = jax.nn.sigmoid(logits)
    biased = scores + rb_ref[...]
    lane = lax.broadcasted_iota(I32, biased.shape, 1)
    lane_f = lane.astype(F32)
    per_group = N_EXPERTS // N_EXPERT_GROUPS
    m1 = _group_allreduce(biased, lane, jnp.maximum)
    i1 = _group_allreduce(jnp.where(biased == m1, lane_f, 1e9), lane, jnp.minimum)
    m2 = _group_allreduce(jnp.where(lane_f == i1, -jnp.inf, biased), lane, jnp.maximum)
    gs = m1 + m2
    grp = (lane // per_group) % N_EXPERT_GROUPS
    beaten = jnp.zeros(biased.shape, F32)
    for k in range(1, N_EXPERT_GROUPS):
        other = pltpu.roll(gs, per_group * k, axis=1)
        other_grp = (grp - k) % N_EXPERT_GROUPS
        wins = jnp.where(other > gs, 1.0, jnp.where(other == gs, jnp.where(other_grp < grp, 1.0, 0.0), 0.0))
        beaten = beaten + wins
    keep = jnp.where(lane < N_EXPERTS, jnp.where(beaten < TOPK_GROUPS, 1.0, 0.0), 0.0)
    masked = jnp.where(keep > 0.0, biased, -jnp.inf)
    idx_out = jnp.zeros(biased.shape, F32)
    w_out = jnp.zeros(biased.shape, F32)
    w_sum = jnp.zeros((biased.shape[0], 1), F32)
    member = jnp.zeros(biased.shape, F32)
    picks = []
    for k in range(TOP_K):
        m = jnp.max(masked, axis=1, keepdims=True)
        ik = jnp.min(jnp.where(masked == m, lane_f, 1e9), axis=1, keepdims=True)
        pick = lane_f == ik
        wk = jnp.sum(jnp.where(pick, scores, 0.0), axis=1, keepdims=True)
        masked = jnp.where(pick, -jnp.inf, masked)
        member = jnp.where(pick, 1.0, member)
        picks.append(pick)
        idx_out = jnp.where(lane == k, ik, idx_out)
        w_out = jnp.where(lane == k, wk, w_out)
        w_sum = w_sum + wk
    member = jnp.where(real, member, 0.0)
    ri = lax.broadcasted_iota(I32, (rows, rows), 0)
    ci = lax.broadcasted_iota(I32, (rows, rows), 1)
    before = jnp.where(ri > ci, 1.0, 0.0).astype(BF16)
    pos = cnt_scr[...] + dotf(before, member.astype(BF16))
    rank_out = jnp.zeros(biased.shape, F32)
    for k in range(TOP_K):
        rk = jnp.sum(jnp.where(picks[k], pos, 0.0), axis=1, keepdims=True)
        rank_out = jnp.where(lane == k, rk, rank_out)

    @pl.when(real)
    def _():
        idx_ref[...] = idx_out.astype(I32)
        wt_ref[...] = w_out / w_sum * ROUTED_SCALE
        rank_ref[...] = rank_out.astype(I32)

    total = cnt_scr[...] + jnp.sum(member, axis=0, keepdims=True)
    cnt_scr[...] = total
    cnt_ref[...] = jnp.broadcast_to(total, cnt_ref.shape)


def _ffnpre(x, g, mod, wr_hi, wr_lo, rb, cnt_in, tm, hp_prev=None, hp_tok=None):
    t, d = x.shape
    r = mod.shape[0]
    n_real = t // tm
    steps = n_real if hp_prev is not None else hp_tok // tm
    row_map = lambda i: (jnp.minimum(i, n_real - 1), 0)
    const = lambda i: (0, 0)
    in_specs = [pl.BlockSpec((tm, d), row_map),
                pl.BlockSpec((1, d), const),
                pl.BlockSpec((r, d), lambda i: (0, 3)),
                pl.BlockSpec((r, d), lambda i: (0, 4)),
                pl.BlockSpec((d, LANES), const),
                pl.BlockSpec((d, LANES), const),
                pl.BlockSpec((1, LANES), const),
                pl.BlockSpec((1, LANES), const)]
    args = [x, g, mod, mod, wr_hi, wr_lo, rb, cnt_in]
    small = [pl.BlockSpec((tm, LANES), row_map), pl.BlockSpec((tm, LANES), row_map),
             pl.BlockSpec((tm, LANES), row_map), pl.BlockSpec((TILE_ROWS, LANES), const)]
    small_shape = [jax.ShapeDtypeStruct((t, LANES), I32), jax.ShapeDtypeStruct((t, LANES), F32),
                   jax.ShapeDtypeStruct((t, LANES), I32), jax.ShapeDtypeStruct((TILE_ROWS, LANES), F32)]
    inner = functools.partial(_ffnpre_body, rows=tm, n_real=n_real)
    n_in = len(args)
    if hp_prev is None:
        out_specs = [pl.BlockSpec((tm * TILE_ROWS, LANES), lambda i: (i, 0))] + small
        out_shape = [jax.ShapeDtypeStruct((hp_tok * TILE_ROWS, LANES), U32)] + small_shape
        body = inner
        aliases = {}
    else:
        start, blk = hp_tok
        in_specs.append(pl.BlockSpec(memory_space=pl.ANY))
        args.append(hp_prev)
        out_specs = [pl.BlockSpec((blk * TILE_ROWS, LANES), lambda i: (start // blk, 0))] + small
        out_shape = [jax.ShapeDtypeStruct(hp_prev.shape, U32)] + small_shape
        body = lambda *refs: inner(*refs[:n_in], *refs[n_in + 1:])
        aliases = {n_in: 0}
    return pl.pallas_call(
        body, grid=(steps,), in_specs=in_specs, out_specs=out_specs, out_shape=out_shape,
        input_output_aliases=aliases, scratch_shapes=[pltpu.VMEM((1, LANES), F32)],
        compiler_params=_cparams(("arbitrary",), 40), name="ffnpre")(*args)


def _tile_rows(r):
    return pl.ds(pl.multiple_of(r * TILE_ROWS, TILE_ROWS), TILE_ROWS)


def _route_body(dest_ref, pstart_ref, cnt_ref, hp_ref, xs_ref, ztile, sem, zsem, *, tm, n_rows):
    i = pl.program_id(0)

    def zero_rows(lo, hi):
        def zcopy(r):
            return pltpu.make_async_copy(ztile, xs_ref.at[_tile_rows(r)], zsem.at[0])

        def start(r, carry):
            zcopy(r).start()
            return carry

        def wait(r, carry):
            zcopy(r).wait()
            return carry
        lax.fori_loop(lo, hi, start, 0)
        lax.fori_loop(lo, hi, wait, 0)

    def padded_end(e):
        cnt = cnt_ref[e]
        return pstart_ref[e] + (((cnt + MOE_ROWS - 1) // MOE_ROWS) * MOE_ROWS)

    @pl.when(i == 0)
    def _():
        ztile[...] = jnp.zeros(ztile.shape, U32)

        def per_expert(e, carry):
            zero_rows(pstart_ref[e] + cnt_ref[e], padded_end(e))
            return carry
        lax.fori_loop(0, N_EXPERTS, per_expert, 0)
        zero_rows(padded_end(N_EXPERTS - 1), n_rows)

    per_tile = tm * TOP_K

    def copy(r):
        row = dest_ref[i * per_tile + r]
        return pltpu.make_async_copy(hp_ref.at[_tile_rows(r // TOP_K)], xs_ref.at[_tile_rows(row)], sem.at[0])

    def start(r, carry):
        copy(r).start()
        return carry

    def wait(r, carry):
        pltpu.make_async_copy(hp_ref.at[_tile_rows(0)], xs_ref.at[_tile_rows(0)], sem.at[0]).wait()
        return carry
    lax.fori_loop(0, per_tile, start, 0, unroll=8)
    lax.fori_loop(0, per_tile, wait, 0, unroll=8)


def _route(dest, pstart, counts, hp, n_tok, tm, n_rows):
    assert n_tok % tm == 0
    grid_spec = pltpu.PrefetchScalarGridSpec(
        num_scalar_prefetch=3, grid=(n_tok // tm,),
        in_specs=[pl.BlockSpec((tm * TILE_ROWS, LANES), lambda i, cd, ps, cn: (i, 0))],
        out_specs=pl.BlockSpec(memory_space=pl.ANY),
        scratch_shapes=[pltpu.VMEM((TILE_ROWS, LANES), U32),
                        pltpu.SemaphoreType.DMA((1,)), pltpu.SemaphoreType.DMA((1,))])
    return pl.pallas_call(
        functools.partial(_route_body, tm=tm, n_rows=n_rows), grid_spec=grid_spec,
        out_shape=jax.ShapeDtypeStruct((n_rows * TILE_ROWS, LANES), U32),
        compiler_params=_cparams(("arbitrary",), 32), name="route")(dest, pstart, counts, hp)


def _moe_body(blk_e_ref, nlive_ref, xs_ref, wg_ref, wu_ref, wd_ref, y_ref, wg_b, wu_b, wd_b):
    b = pl.program_id(0)
    live = b < nlive_ref[0]
    e = blk_e_ref[b]
    e_prev = blk_e_ref[jnp.maximum(b - 1, 0)]

    @pl.when(jnp.logical_and(live, jnp.logical_or(b == 0, e != e_prev)))
    def _():
        wg_b[...] = wg_ref[0].astype(BF16)
        wu_b[...] = wu_ref[0].astype(BF16)
        wd_b[...] = wd_ref[0].astype(BF16)

    @pl.when(live)
    def _():
        lo, hi = _unpack_pair(_load_token_tiles(xs_ref, MOE_ROWS))
        x = jnp.concatenate([lo, hi], axis=1).astype(BF16)
        dotf = functools.partial(jnp.dot, preferred_element_type=F32)
        act = (_silu(dotf(x, wg_b[...])) * dotf(x, wu_b[...])).astype(BF16)
        y = dotf(act, wd_b[...])
        half = D_MODEL // 2
        _store_token_tiles(y_ref, _pack_pair(y[:, :half], y[:, half:]), MOE_ROWS)

    @pl.when(jnp.logical_not(live))
    def _():
        y_ref[...] = jnp.zeros(y_ref.shape, U32)


def _moe(blk_e, n_live, xs, wg, wu, wd):
    n_blk = blk_e.shape[0]
    blk_rows = MOE_ROWS * TILE_ROWS
    grid_spec = pltpu.PrefetchScalarGridSpec(
        num_scalar_prefetch=2, grid=(n_blk,),
        in_specs=[pl.BlockSpec((blk_rows, LANES), lambda b, be, nl: (b, 0)),
                  pl.BlockSpec((1, D_MODEL, D_EXPERT), lambda b, be, nl: (be[b], 0, 0)),
                  pl.BlockSpec((1, D_MODEL, D_EXPERT), lambda b, be, nl: (be[b], 0, 0)),
                  pl.BlockSpec((1, D_EXPERT, D_MODEL), lambda b, be, nl: (be[b], 0, 0))],
        out_specs=pl.BlockSpec((blk_rows, LANES), lambda b, be, nl: (b, 0)),
        scratch_shapes=[pltpu.VMEM((D_MODEL, D_EXPERT), BF16), pltpu.VMEM((D_MODEL, D_EXPERT), BF16),
                        pltpu.VMEM((D_EXPERT, D_MODEL), BF16)])
    return pl.pallas_call(
        _moe_body, grid_spec=grid_spec,
        out_shape=jax.ShapeDtypeStruct(xs.shape, U32),
        compiler_params=_cparams(("arbitrary",), 48), name="moe")(blk_e, n_live, xs, wg, wu, wd)


def _expert_layout(cnt_total, n_assign):
    assert n_assign % MOE_ROWS == 0
    counts = cnt_total[0, :N_EXPERTS].astype(I32)
    padded = (counts + MOE_ROWS - 1) // MOE_ROWS * MOE_ROWS
    pends = jnp.cumsum(padded)
    pstart = (pends - padded).astype(I32)
    n_blk = n_assign // MOE_ROWS + N_EXPERTS
    blk_start = jnp.arange(n_blk, dtype=I32) * MOE_ROWS
    blk_e = jnp.minimum(jnp.sum((pends[None, :] <= blk_start[:, None]).astype(I32), axis=1), N_EXPERTS - 1)
    n_live = (pends[-1] // MOE_ROWS).astype(I32).reshape(1)
    return blk_e.astype(I32), n_live, pstart, counts, n_blk * MOE_ROWS


def _combine_body(dest_ref, x_ref, y_ref, hp_ref, wt_ref, wgs_ref, wus_ref, wds_ref, g_ref, nf_ref,
                  o_ref, gbuf, sem, *, tm, tok0):
    i = pl.program_id(0)
    n = pl.num_programs(0)
    slot = i % 2
    per_tile = tm * TOP_K

    def copy(j, r, sl):
        row = dest_ref[(tok0 + j * tm) * TOP_K + r]
        return pltpu.make_async_copy(y_ref.at[_tile_rows(row)], gbuf.at[sl, _tile_rows(r)], sem.at[sl])

    def for_assignments(fn):
        def step(r, carry):
            fn(r)
            return carry
        lax.fori_loop(0, per_tile, step, 0, unroll=8)

    @pl.when(i == 0)
    def _():
        for_assignments(lambda r: copy(0, r, 0).start())

    @pl.when(i + 1 < n)
    def _():
        for_assignments(lambda r: copy(i + 1, r, 1 - slot).start())

    for_assignments(lambda r: pltpu.make_async_copy(
        y_ref.at[_tile_rows(0)], gbuf.at[slot, _tile_rows(0)], sem.at[slot]).wait())
    wts = wt_ref[...]
    acc_lo = [None] * TILE_ROWS
    acc_hi = [None] * TILE_ROWS
    stride = TOP_K * TILE_ROWS
    for k in range(TOP_K):
        wk = wts[:, k:k + 1]
        for s in range(TILE_ROWS):
            l_k, h_k = _unpack_pair(gbuf[slot, pl.ds(k * TILE_ROWS + s, tm, stride=stride), :])
            acc_lo[s] = l_k * wk if acc_lo[s] is None else acc_lo[s] + l_k * wk
            acc_hi[s] = h_k * wk if acc_hi[s] is None else acc_hi[s] + h_k * wk
    routed = jnp.concatenate(acc_lo + acc_hi, axis=1)
    hl, hh = _unpack_pair(_load_token_tiles(hp_ref, tm))
    h = jnp.concatenate([hl, hh], axis=1).astype(BF16)
    dotf = functools.partial(jnp.dot, preferred_element_type=F32)
    act = (_silu(dotf(h, wgs_ref[...])) * dotf(h, wus_ref[...])).astype(BF16)
    shared = dotf(act, wds_ref[...])
    xo = x_ref[...] + g_ref[...] * (routed + shared)
    r = lax.rsqrt(jnp.mean(xo * xo, axis=-1, keepdims=True) + RMS_EPS)
    o_ref[...] = (xo * r) * nf_ref[...]


def _combine(dest, x1, ys, hp, wts, wgs, wus, wds, mod, norm_final, tm, tok0):
    t, d = x1.shape
    r = mod.shape[0]
    o = tok0 // tm
    const = lambda i, ds: (0, 0)
    grid_spec = pltpu.PrefetchScalarGridSpec(
        num_scalar_prefetch=1, grid=(t // tm,),
        in_specs=[pl.BlockSpec((tm, d), lambda i, ds: (i, 0)),
                  pl.BlockSpec(memory_space=pl.ANY),
                  pl.BlockSpec((tm * TILE_ROWS, LANES), lambda i, ds: (i + o, 0)),
                  pl.BlockSpec((tm, LANES), lambda i, ds: (i, 0)),
                  pl.BlockSpec((d, D_EXPERT), const),
                  pl.BlockSpec((d, D_EXPERT), const),
                  pl.BlockSpec((D_EXPERT, d), const),
                  pl.BlockSpec((r, d), lambda i, ds: (0, 5)),
                  pl.BlockSpec((1, d), const)],
        out_specs=pl.BlockSpec((tm, d), lambda i, ds: (i, 0)),
        scratch_shapes=[pltpu.VMEM((2, tm * TOP_K * TILE_ROWS, LANES), U32), pltpu.SemaphoreType.DMA((2,))])
    return pl.pallas_call(
        functools.partial(_combine_body, tm=tm, tok0=tok0), grid_spec=grid_spec,
        out_shape=jax.ShapeDtypeStruct((t, d), F32),
        compiler_params=_cparams(("arbitrary",), 48), name="combine")(
            dest, x1, ys, hp, wts, wgs, wus, wds, mod, norm_final)


PAGES_PER_STEP = 16


def _pagesum_body(pt_ref, *refs):
    ins = refs[:PAGES_PER_STEP]
    o_ref = refs[PAGES_PER_STEP]
    ppb = MOBA_BLOCK // PAGE_SIZE
    for j in range(PAGES_PER_STEP // ppb):
        acc = None
        for p in range(ppb):
            s = jnp.sum(ins[j * ppb + p][0].reshape(PAGE_SIZE, N_HEADS, HEAD_DIM), axis=0)
            acc = s if acc is None else acc + s
        for h in range(N_HEADS):
            o_ref[0, j:j + 1, h * HEAD_DIM:(h + 1) * HEAD_DIM] = acc[h:h + 1, :]


def _pagesum(page_table_flat, kpool, n_batch, n_pages):
    w = N_HEADS * HEAD_DIM
    ppb = MOBA_BLOCK // PAGE_SIZE
    steps = n_pages // PAGES_PER_STEP

    def in_map(p):
        return lambda b, j, pt: (pt[b * n_pages + j * PAGES_PER_STEP + p], 0, 0)

    grid_spec = pltpu.PrefetchScalarGridSpec(
        num_scalar_prefetch=1, grid=(n_batch, steps),
        in_specs=[pl.BlockSpec((1, PAGE_SIZE * N_HEADS, HEAD_DIM), in_map(p)) for p in range(PAGES_PER_STEP)],
        out_specs=pl.BlockSpec((1, PAGES_PER_STEP // ppb, w), lambda b, j, pt: (b, j, 0)))
    return pl.pallas_call(
        _pagesum_body, grid_spec=grid_spec,
        out_shape=jax.ShapeDtypeStruct((n_batch, n_pages // ppb, w), F32),
        compiler_params=_cparams(("arbitrary", "arbitrary"), 40), name="pagesum")(
            page_table_flat, *([kpool] * PAGES_PER_STEP))


def _gate3_body(q_ref, ks_ref, sel_ref):
    prod = ks_ref[0] * q_ref[0] * (1.0 / MOBA_BLOCK)
    g = jnp.concatenate([jnp.sum(prod[:, h * HEAD_DIM:(h + 1) * HEAD_DIM], axis=1, keepdims=True)
                         for h in range(N_HEADS)], axis=1)
    rowi = lax.broadcasted_iota(I32, g.shape, 0).astype(F32)
    picks = []
    for _ in range(MOBA_TOPK):
        m = jnp.max(g, axis=0, keepdims=True)
        ik = jnp.min(jnp.where(g == m, rowi, 1e9), axis=0, keepdims=True)
        g = jnp.where(rowi == ik, -jnp.inf, g)
        picks.append(ik)
    picks.append(jnp.zeros((8 - MOBA_TOPK, N_HEADS), F32))
    sel_ref[0] = jnp.concatenate(picks, axis=0).astype(I32)


def _gate3(q3, ksum):
    n_batch, nb, w = ksum.shape
    return pl.pallas_call(
        _gate3_body, grid=(n_batch,),
        in_specs=[pl.BlockSpec((1, 1, w), lambda b: (b, 0, 0)),
                  pl.BlockSpec((1, nb, w), lambda b: (b, 0, 0))],
        out_specs=pl.BlockSpec((1, 8, N_HEADS), lambda b: (b, 0, 0)),
        out_shape=jax.ShapeDtypeStruct((n_batch, 8, N_HEADS), I32),
        compiler_params=_cparams(("arbitrary",), 32), name="gate3")(q3, ksum)


def _decode_body(phys_ref, blk_ref, q_ref, kn_ref, vn_ref, *refs, last_block):
    ppb = MOBA_BLOCK // PAGE_SIZE
    n_pg = MOBA_TOPK * ppb
    k_refs = refs[:n_pg]
    v_refs = refs[n_pg:2 * n_pg]
    tab_ref = refs[2 * n_pg]
    o_ref = refs[2 * n_pg + 1]
    b = pl.program_id(0)
    h = pl.program_id(1)
    q = q_ref[0, 0]
    scale = HEAD_DIM ** -0.5
    qb = jnp.broadcast_to(q, (8, HEAD_DIM)).astype(BF16)
    dg = functools.partial(lax.dot_general, dimension_numbers=NT_DIMS, preferred_element_type=F32)
    s_self = jnp.sum(q * kn_ref[0, 0], axis=1, keepdims=True) * scale + tab_ref[0, 2:3, 0:1]
    scores = []
    vals = []
    head_rows = pl.ds(h, PAGE_SIZE, stride=N_HEADS)
    for s in range(MOBA_TOPK):
        kblk = jnp.concatenate([k_refs[s * ppb + p][0, head_rows, :] for p in range(ppb)], axis=0).astype(BF16)
        vals.append(jnp.concatenate([v_refs[s * ppb + p][0, head_rows, :] for p in range(ppb)], axis=0).astype(BF16))
        n = blk_ref[(b * N_HEADS + h) * MOBA_TOPK + s]
        bias = jnp.where(n == last_block, tab_ref[0, 0:1, :], tab_ref[0, 1:2, :])
        scores.append(dg(qb, kblk)[0:1, :] * scale + bias)
    m = s_self
    for sc in scores:
        m = jnp.maximum(m, jnp.max(sc, axis=1, keepdims=True))
    p_self = jnp.exp(s_self - m)
    l = p_self
    acc = p_self * vn_ref[0, 0]
    for sc, vblk in zip(scores, vals):
        p = jnp.exp(sc - m)
        l = l + jnp.sum(p, axis=1, keepdims=True)
        pb = jnp.broadcast_to(p, (8, MOBA_BLOCK)).astype(BF16)
        acc = acc + jnp.dot(pb, vblk, preferred_element_type=F32)[0:1, :]
    o_ref[0, 0] = acc * (1.0 / l)


def _decode(phys, blks, q4, kn4, vn4, kpool, vpool, tabs, last_block):
    n_batch = q4.shape[0]
    ppb = MOBA_BLOCK // PAGE_SIZE
    n_pg = MOBA_TOPK * ppb
    hd = HEAD_DIM

    def pg_map(i):
        return lambda b, h, ph, bl: (ph[(b * N_HEADS + h) * n_pg + i], 0, 0)

    vec = pl.BlockSpec((1, 1, 1, hd), lambda b, h, ph, bl: (b, h, 0, 0))
    page = (1, PAGE_SIZE * N_HEADS, hd)
    grid_spec = pltpu.PrefetchScalarGridSpec(
        num_scalar_prefetch=2, grid=(n_batch, N_HEADS),
        in_specs=[vec, vec, vec]
        + [pl.BlockSpec(page, pg_map(i)) for i in range(n_pg)]
        + [pl.BlockSpec(page, pg_map(i)) for i in range(n_pg)]
        + [pl.BlockSpec((1, 8, MOBA_BLOCK), lambda b, h, ph, bl: (h, 0, 0))],
        out_specs=vec)
    return pl.pallas_call(
        functools.partial(_decode_body, last_block=last_block), grid_spec=grid_spec,
        out_shape=jax.ShapeDtypeStruct((n_batch, N_HEADS, 1, hd), F32),
        compiler_params=_cparams(("arbitrary", "arbitrary"), 32), name="decode")(
            phys, blks, q4, kn4, vn4, *([kpool] * n_pg), *([vpool] * n_pg), tabs)


def _sconv_body(p_ref, cs_ref, cw_ref, cb_ref, dtb_ref, alog_ref, x_ref, b_ref, c_ref, dt_ref, da_ref, nc_ref):
    xbc = p_ref[:, COL_X:COL_X + CONV_DIM]
    cs = cs_ref[...]
    conv = cb_ref[...] + xbc * cw_ref[CONV_WIDTH - 1:CONV_WIDTH, :]
    for i in range(CONV_WIDTH - 1):
        conv = conv + cs[:, i * CONV_DIM:(i + 1) * CONV_DIM] * cw_ref[i:i + 1, :]
    nc_ref[:, 0:(CONV_WIDTH - 2) * CONV_DIM] = cs[:, CONV_DIM:]
    nc_ref[:, (CONV_WIDTH - 2) * CONV_DIM:] = xbc
    act = _silu(conv)
    gn = N_SSD_GROUPS * SSD_STATE
    x_ref[...] = act[:, :SSD_WIDTH]
    b_ref[...] = act[:, SSD_WIDTH:SSD_WIDTH + gn]
    c_ref[...] = act[:, SSD_WIDTH + gn:]
    dtv = p_ref[:, COL_DT:COL_DT + LANES] + dtb_ref[...]
    dt = jnp.maximum(dtv, 0.0) + jnp.log1p(jnp.exp(-jnp.abs(dtv)))
    dt_ref[...] = dt
    da_ref[...] = jnp.exp(dt * (-jnp.exp(alog_ref[...])))


def _sconv(proj_s, conv_state2, conv_w, conv_b, dt_bias_p, a_log_p):
    n = proj_s.shape[0]
    gn = N_SSD_GROUPS * SSD_STATE
    shapes = [(n, SSD_WIDTH), (n, gn), (n, gn), (n, LANES), (n, LANES), (n, (CONV_WIDTH - 1) * CONV_DIM)]
    return pl.pallas_call(
        _sconv_body,
        out_shape=[jax.ShapeDtypeStruct(s, F32) for s in shapes],
        compiler_params=pltpu.CompilerParams(vmem_limit_bytes=32 * 1024 * 1024), name="sconv")(
            proj_s, conv_state2, conv_w, conv_b, dt_bias_p, a_log_p)


def _sstate_body(st_ref, x_ref, dt_ref, da_ref, b_ref, c_ref, ns_ref, y_ref):
    hpg = N_SSD_HEADS // N_SSD_GROUPS
    for g in range(N_SSD_GROUPS):
        hs = slice(g * hpg, (g + 1) * hpg)
        st = st_ref[0, hs]
        xdt = x_ref[0, hs] * dt_ref[0, hs]
        ns = st * da_ref[0, hs] + xdt * b_ref[0, g]
        ns_ref[0, hs] = ns
        y_ref[0, hs] = jnp.sum(ns * c_ref[0, g], axis=-1, keepdims=True)


def _sstate(state, x4, dt4, da4, b4, c4):
    n = state.shape[0]
    h, p, s = N_SSD_HEADS, SSD_HEAD_DIM, SSD_STATE
    g = N_SSD_GROUPS
    m4 = lambda b: (b, 0, 0, 0)
    return pl.pallas_call(
        _sstate_body, grid=(n,),
        in_specs=[pl.BlockSpec((1, h, p, s), m4), pl.BlockSpec((1, h, p, 1), m4),
                  pl.BlockSpec((1, h, 1, 1), m4), pl.BlockSpec((1, h, 1, 1), m4),
                  pl.BlockSpec((1, g, 1, s), m4), pl.BlockSpec((1, g, 1, s), m4)],
        out_specs=[pl.BlockSpec((1, h, p, s), m4), pl.BlockSpec((1, h, p, 1), m4)],
        out_shape=[jax.ShapeDtypeStruct((n, h, p, s), F32), jax.ShapeDtypeStruct((n, h, p, 1), F32)],
        compiler_params=_cparams(("arbitrary",), 32), name="sstate")(state, x4, dt4, da4, b4, c4)


def _sgate_body(y_ref, x_ref, p_ref, dsk_ref, nrm_ref, o_ref):
    y = y_ref[...] + dsk_ref[...] * x_ref[...]
    y = y * _silu(p_ref[:, COL_Z:COL_Z + SSD_WIDTH])
    r = lax.rsqrt(jnp.mean(y * y, axis=-1, keepdims=True) + RMS_EPS)
    o_ref[...] = ((y * r) * nrm_ref[...]).astype(o_ref.dtype)


def _sgate(y, xs, proj_s, dskip_w, ssd_norm):
    return pl.pallas_call(
        _sgate_body, out_shape=jax.ShapeDtypeStruct(y.shape, BF16),
        compiler_params=pltpu.CompilerParams(vmem_limit_bytes=32 * 1024 * 1024), name="sgate")(
            y, xs, proj_s, dskip_w, ssd_norm)


def kernel(x_prompt, x_sample, cache_k, cache_v, state_ssm, state_conv, page_table, c_prompt, c_sample,
           w_ada, b_ada, norm_mix, norm_ffn, w_in, conv_w, conv_b, dt_bias, a_log, d_skip, ssd_norm,
           w_out, rel_bias, w_router, router_bias, w_gate_e, w_up_e, w_down_e, w_gate_s, w_up_s, w_down_s,
           norm_final):
    depth = w_ada.shape[0]
    assert depth == 1
    bp, sp, d = x_prompt.shape
    bs, sn, _ = x_sample.shape
    assert bp == 1 and sn == 1 and d == D_MODEL
    n_pages = page_table.shape[1]
    past_len = n_pages * PAGE_SIZE
    nb_prompt = sp // MOBA_BLOCK
    nb_past = past_len // MOBA_BLOCK
    l = 0
    layer = lambda a: a.reshape(a.shape[1:])

    w_in_b = jnp.pad(layer(w_in).astype(BF16), ((0, 0), (0, IN_COLS_PAD - IN_COLS)))
    w_out_b = layer(w_out).astype(BF16)
    wgs_b, wus_b, wds_b = (layer(w).astype(BF16) for w in (w_gate_s, w_up_s, w_down_s))
    wr2 = jnp.concatenate([layer(w_router), layer(w_router)], axis=1)
    wr_hi = wr2.astype(BF16)
    wr_lo = (wr2 - wr_hi.astype(F32)).astype(BF16)
    rb2 = jnp.concatenate([router_bias[l], router_bias[l]])[None, :].astype(F32)
    g_mix = norm_mix[l][None, :]
    g_ffn = norm_ffn[l][None, :]
    g_fin = norm_final[None, :]
    pad_h = LANES - N_SSD_HEADS
    dt_bias_p = jnp.pad(dt_bias[l].astype(F32), (0, pad_h))[None, :]
    a_log_p = jnp.pad(a_log[l].astype(F32), (0, pad_h))[None, :]
    dskip_w = jnp.repeat(d_skip[l].astype(F32), SSD_HEAD_DIM)[None, :]
    nrm_ssd = ssd_norm[l][None, :]
    cw = conv_w[l]
    cbias = conv_b[l][None, :]
    expand = _head_expand()
    expand_t = expand.T
    own_bias, prev_bias = _prompt_bias_tables(rel_bias)
    rel_t = rel_bias.T

    c_all = jnp.concatenate([c_prompt, jnp.zeros((8 - bp, d), F32), c_sample], axis=0)
    mod = _adaln(c_all, layer(w_ada), b_ada[l][None, :])
    mod_p = mod[0:1]
    mod_s = mod[8:8 + bs]

    xp = x_prompt.reshape(sp, d)
    proj = _inproj(xp, g_mix, mod_p, w_in_b, 1024)
    k_p, v_p, kaug, v_t, kmean = _kvpost(proj)
    kmean_pad = jnp.pad(kmean.reshape(nb_prompt, ATT_WIDTH), ((0, LANES - nb_prompt), (0, 0)))
    att = _moba_prompt(proj, kaug, v_t, kmean_pad, own_bias, prev_bias)
    ssd, ssm_p, conv_tail = _ssd_prompt(proj, cw, cbias, dt_bias_p, a_log_p, dskip_w, nrm_ssd, expand, expand_t)
    xp1 = _outproj(att, ssd, w_out_b, xp, mod_p, 1024)

    xs = x_sample.reshape(bs, d)
    proj_s = _inproj(xs, g_mix, mod_s, w_in_b, bs)
    q_s = proj_s[:, COL_Q:COL_Q + ATT_WIDTH]
    k_s = proj_s[:, COL_K:COL_K + ATT_WIDTH]
    v_s = proj_s[:, COL_V:COL_V + ATT_WIDTH]
    kpool = cache_k.reshape(cache_k.shape[1], PAGE_SIZE * N_HEADS, HEAD_DIM)
    vpool = cache_v.reshape(cache_v.shape[1], PAGE_SIZE * N_HEADS, HEAD_DIM)
    ksum = _pagesum(page_table.reshape(-1), kpool, bs, n_pages)
    sel = _gate3(q_s.reshape(bs, 1, ATT_WIDTH), ksum)
    blks = jnp.transpose(sel[:, :MOBA_TOPK, :], (0, 2, 1))
    ppb = MOBA_BLOCK // PAGE_SIZE
    lpage = blks[..., None] * ppb + jnp.arange(ppb, dtype=I32)
    phys = jnp.take_along_axis(page_table, lpage.reshape(bs, -1), axis=1)
    dist_last = past_len - ((nb_past - 1) * MOBA_BLOCK + jnp.arange(MOBA_BLOCK))
    tab_last = _bias_lookup(rel_t, _rel_bucket(dist_last))
    tab_far = jnp.broadcast_to(rel_t[:, NUM_BUCKETS - 1:NUM_BUCKETS], tab_last.shape)
    tab_self = jnp.broadcast_to(rel_t[:, 0:1], tab_last.shape)
    tabs = jnp.stack([tab_last, tab_far, tab_self] + [jnp.zeros_like(tab_last)] * 5, axis=1).astype(F32)
    shape4 = (bs, N_HEADS, 1, HEAD_DIM)
    att_s = _decode(phys.reshape(-1).astype(I32), blks.reshape(-1).astype(I32), q_s.reshape(shape4),
                    k_s.reshape(shape4), v_s.reshape(shape4), kpool, vpool, tabs, nb_past - 1)
    att_s = att_s.reshape(bs, ATT_WIDTH).astype(BF16)

    xs_act, b_s, c_s, dt_s, da_s, conv_s = _sconv(
        proj_s, state_conv.reshape(bs, (CONV_WIDTH - 1) * CONV_DIM), cw, cbias, dt_bias_p, a_log_p)
    ssm_s, y_s = _sstate(
        layer(state_ssm), xs_act.reshape(bs, N_SSD_HEADS, SSD_HEAD_DIM, 1),
        dt_s[:, :N_SSD_HEADS].reshape(bs, N_SSD_HEADS, 1, 1), da_s[:, :N_SSD_HEADS].reshape(bs, N_SSD_HEADS, 1, 1),
        b_s.reshape(bs, N_SSD_GROUPS, 1, SSD_STATE), c_s.reshape(bs, N_SSD_GROUPS, 1, SSD_STATE))
    ssd_s = _sgate(y_s.reshape(bs, SSD_WIDTH), xs_act, proj_s, dskip_w, nrm_ssd)
    xs1 = _outproj(att_s, ssd_s, w_out_b, xs, mod_s, bs)

    n_tok = sp + bs
    n_assign = n_tok * TOP_K
    ffn_tm = 256
    assert sp % ffn_tm == 0 and sp % bs == 0 and bs <= ffn_tm
    cnt0 = jnp.zeros((1, LANES), F32)
    hp, idx_p, wt_p, rank_p, cnt_p = _ffnpre(xp1, g_ffn, mod_p, wr_hi, wr_lo, rb2, cnt0, ffn_tm,
                                             hp_tok=sp + ffn_tm)
    hp, idx_s, wt_s, rank_s, cnt_all = _ffnpre(xs1, g_ffn, mod_s, wr_hi, wr_lo, rb2, cnt_p[0:1], bs,
                                               hp_prev=hp, hp_tok=(sp, bs))
    idx = jnp.concatenate([idx_p[:, :TOP_K], idx_s[:, :TOP_K]], axis=0)
    rank = jnp.concatenate([rank_p[:, :TOP_K], rank_s[:, :TOP_K]], axis=0)
    blk_e, n_live, pstart, counts, n_rows = _expert_layout(cnt_all, n_assign)
    experts = jnp.arange(N_EXPERTS, dtype=I32)
    dest = (rank + jnp.sum(jnp.where(idx[..., None] == experts, pstart, 0), axis=-1)).reshape(-1).astype(I32)
    x_sorted = _route(dest, pstart, counts, hp, n_tok, bs, n_rows)
    y_sorted = _moe(blk_e, n_live, x_sorted, layer(w_gate_e), layer(w_up_e), layer(w_down_e))
    y_p = _combine(dest, xp1, y_sorted, hp, wt_p, wgs_b, wus_b, wds_b, mod_p, g_fin, 128, 0)
    y_s_out = _combine(dest, xs1, y_sorted, hp, wt_s, wgs_b, wus_b, wds_b, mod_s, g_fin, bs, sp)

    kv_shape = (1, bp, sp, N_HEADS, HEAD_DIM)
    kvs_shape = (1, bs, sn, N_HEADS, HEAD_DIM)
    return (y_p.reshape(bp, sp, d), y_s_out.reshape(bs, sn, d),
            k_p.reshape(kv_shape), v_p.reshape(kv_shape),
            k_s.reshape(kvs_shape), v_s.reshape(kvs_shape),
            ssm_p.reshape(1, bp, N_SSD_HEADS, SSD_HEAD_DIM, SSD_STATE),
            ssm_s.reshape(1, bs, N_SSD_HEADS, SSD_HEAD_DIM, SSD_STATE),
            conv_tail[8 - (CONV_WIDTH - 1):].reshape(1, bp, CONV_WIDTH - 1, CONV_DIM),
            conv_s.reshape(1, bs, CONV_WIDTH - 1, CONV_DIM))
```

```python
import functools
import math

import jax
import jax.numpy as jnp
from jax import lax
from jax.experimental import pallas as pl
from jax.experimental.pallas import tpu as pltpu

F32 = jnp.float32
BF16 = jnp.bfloat16
I32 = jnp.int32
U32 = jnp.uint32

D_MODEL = 2048
ATT_WIDTH = 1024
HEAD_DIM = 128
N_HEADS = 8
MOBA_BLOCK = 256
MOBA_TOPK = 3
Q_BLOCK = 128
NUM_BUCKETS = 32
MAX_DISTANCE = 128
SSD_WIDTH = 1024
SSD_HEAD_DIM = 64
N_SSD_HEADS = 16
N_SSD_GROUPS = 2
SSD_STATE = 128
CONV_WIDTH = 4
SSD_CHUNK = 256
CONV_DIM = SSD_WIDTH + 2 * N_SSD_GROUPS * SSD_STATE
N_EXPERTS = 64
TOP_K = 8
N_EXPERT_GROUPS = 8
TOPK_GROUPS = 4
D_EXPERT = 512
ROUTED_SCALE = 2.5
RMS_EPS = 1e-6
PAGE_SIZE = 128

LANES = 128
TILE_ROWS = 8
LOG2E = 1.4426950408889634
MOBA_GROUP = 8
COL_Q, COL_K, COL_V, COL_Z, COL_X = 0, 1024, 2048, 3072, 4096
COL_B, COL_C, COL_DT = 5120, 5376, 5632
IN_COLS = 5648
IN_COLS_PAD = 6144
PROJ_TN = 512
MOE_ROWS = 256
MASK_BIG = 2.0 ** 40
NT_DIMS = (((1,), (1,)), ((), ()))


def _cparams(sem, vmem_mb):
    return pltpu.CompilerParams(dimension_semantics=sem, vmem_limit_bytes=vmem_mb * 1024 * 1024)


def _silu(x):
    return x * jax.nn.sigmoid(x)


def _norm_mod(x, g, shift, scale):
    r = lax.rsqrt(jnp.mean(x * x, axis=-1, keepdims=True) + RMS_EPS)
    return (x * r) * g * (1.0 + scale) + shift


def _split2(x):
    hi = x.astype(BF16)
    lo = (x - hi.astype(F32)).astype(BF16)
    return hi, lo


def _split3(x):
    h1 = x.astype(BF16)
    r1 = x - h1.astype(F32)
    h2 = r1.astype(BF16)
    h3 = (r1 - h2.astype(F32)).astype(BF16)
    return h1, h2, h3


def _pack_pair(lo, hi):
    lo_b = lax.bitcast_convert_type(lo.astype(BF16).astype(F32), U32) >> 16
    hi_b = lax.bitcast_convert_type(hi.astype(BF16).astype(F32), U32) & jnp.uint32(0xFFFF0000)
    return lo_b | hi_b


def _unpack_pair(u):
    lo = lax.bitcast_convert_type(u << 16, F32)
    hi = lax.bitcast_convert_type(u & jnp.uint32(0xFFFF0000), F32)
    return lo, hi


def _adaln_body(c_ref, w_ref, b_ref, o_ref):
    s = _silu(c_ref[...])
    o_ref[...] = jnp.dot(s.astype(BF16), w_ref[...].astype(BF16), preferred_element_type=F32) + b_ref[...]


def _adaln(c_all, w, b):
    r, d = c_all.shape
    n = w.shape[1]
    tn = 1024
    return pl.pallas_call(
        _adaln_body, grid=(n // tn,),
        in_specs=[pl.BlockSpec((r, d), lambda j: (0, 0)),
                  pl.BlockSpec((d, tn), lambda j: (0, j)),
                  pl.BlockSpec((1, tn), lambda j: (0, j))],
        out_specs=pl.BlockSpec((r, tn), lambda j: (0, j)),
        out_shape=jax.ShapeDtypeStruct((r, n), F32),
        compiler_params=_cparams(("arbitrary",), 40), name="adaln")(c_all, w, b)


def _inproj_body(x_ref, g_ref, sh_ref, sc_ref, w_ref, o_ref, h_scr):
    @pl.when(pl.program_id(1) == 0)
    def _():
        h_scr[...] = _norm_mod(x_ref[...], g_ref[...], sh_ref[...], sc_ref[...]).astype(BF16)

    o_ref[...] = jnp.dot(h_scr[...], w_ref[...], preferred_element_type=F32)


def _inproj(x, g, mod, w_pad, tm):
    t, d = x.shape
    r = mod.shape[0]
    n = w_pad.shape[1]
    return pl.pallas_call(
        _inproj_body, grid=(t // tm, n // PROJ_TN),
        in_specs=[pl.BlockSpec((tm, d), lambda i, j: (i, 0)),
                  pl.BlockSpec((1, d), lambda i, j: (0, 0)),
                  pl.BlockSpec((r, d), lambda i, j: (0, 0)),
                  pl.BlockSpec((r, d), lambda i, j: (0, 1)),
                  pl.BlockSpec((d, PROJ_TN), lambda i, j: (0, j))],
        out_specs=pl.BlockSpec((tm, PROJ_TN), lambda i, j: (i, j)),
        out_shape=jax.ShapeDtypeStruct((t, n), F32),
        scratch_shapes=[pltpu.VMEM((tm, d), BF16)],
        compiler_params=_cparams(("arbitrary", "arbitrary"), 48), name="inproj")(x, g, mod, mod, w_pad)


def _kvpost_body(k_ref, v_ref, ko, vo, kao, vto, kmo):
    k = k_ref[...]
    v = v_ref[...]
    ko[...] = k
    vo[...] = v
    rows = k.shape[0]
    row = lax.broadcasted_iota(I32, (rows, LANES), 0)
    lane = lax.broadcasted_iota(I32, (rows, LANES), 1)
    blk = pl.program_id(0) * (rows // MOBA_BLOCK) + row // MOBA_BLOCK
    onehot = jnp.where(lane == blk, 1.0, 0.0).astype(BF16)
    kb = k.astype(BF16)
    pieces = []
    for h in range(N_HEADS):
        pieces += [kb[:, h * HEAD_DIM:(h + 1) * HEAD_DIM], onehot]
    kao[...] = jnp.concatenate(pieces, axis=1)
    vto[...] = v.T.astype(BF16)
    for i in range(rows // MOBA_BLOCK):
        kmo[i] = jnp.sum(k[i * MOBA_BLOCK:(i + 1) * MOBA_BLOCK], axis=0, keepdims=True) * (1.0 / MOBA_BLOCK)


def _kvpost(proj):
    t = proj.shape[0]
    r = 512
    w = ATT_WIDTH
    nb = t // MOBA_BLOCK
    return pl.pallas_call(
        _kvpost_body, grid=(t // r,),
        in_specs=[pl.BlockSpec((r, w), lambda i: (i, COL_K // w)),
                  pl.BlockSpec((r, w), lambda i: (i, COL_V // w))],
        out_specs=[pl.BlockSpec((r, w), lambda i: (i, 0)),
                   pl.BlockSpec((r, w), lambda i: (i, 0)),
                   pl.BlockSpec((r, 2 * w), lambda i: (i, 0)),
                   pl.BlockSpec((w, r), lambda i: (0, i)),
                   pl.BlockSpec((r // MOBA_BLOCK, 1, w), lambda i: (i, 0, 0))],
        out_shape=[jax.ShapeDtypeStruct((t, w), F32), jax.ShapeDtypeStruct((t, w), F32),
                   jax.ShapeDtypeStruct((t, 2 * w), BF16), jax.ShapeDtypeStruct((w, t), BF16),
                   jax.ShapeDtypeStruct((nb, 1, w), F32)],
        compiler_params=_cparams(("arbitrary",), 40), name="kvpost")(proj, proj)


def _rel_bucket(dist):
    d = jnp.maximum(dist, 0)
    max_exact = NUM_BUCKETS // 2
    ratio = jnp.log(jnp.maximum(d, 1).astype(F32) / max_exact) / math.log(MAX_DISTANCE / max_exact)
    large = jnp.minimum(max_exact + (ratio * (NUM_BUCKETS - max_exact)).astype(I32), NUM_BUCKETS - 1)
    return jnp.where(d < max_exact, d, large)


def _bias_lookup(rel_t, bucket):
    out = jnp.zeros((rel_t.shape[0],) + bucket.shape, F32)
    for b in range(NUM_BUCKETS):
        out = jnp.where(bucket[None] == b, rel_t[:, b].reshape((-1,) + (1,) * bucket.ndim), out)
    return out


def _prompt_bias_tables(rel_bias):
    rel_t = rel_bias.T
    far = rel_t[:, NUM_BUCKETS - 1][:, None, None]
    j = jnp.arange(MOBA_BLOCK)[:, None]
    i = jnp.arange(MOBA_BLOCK)[None, :]
    d_own = i - j
    own = jnp.where(d_own >= 0, (_bias_lookup(rel_t, _rel_bucket(d_own)) - far) * LOG2E, -MASK_BIG)
    prev = (_bias_lookup(rel_t, _rel_bucket(d_own + MOBA_BLOCK)) - far) * LOG2E
    return own.astype(F32), prev.astype(F32)


def _moba_body(q_ref, ka_ref, vt_ref, km_ref, ob_ref, pb_ref, o_ref, s_buf):
    own = pl.program_id(1)
    nq = MOBA_BLOCK
    hd = HEAD_DIM
    dotf = functools.partial(jnp.dot, preferred_element_type=F32)
    q_t = q_ref[...].T
    q_hi, q_lo = _split2(q_t)
    km_hi, km_lo = _split2(km_ref[...])
    gate = dotf(km_hi, q_hi) + dotf(km_hi, q_lo) + dotf(km_lo, q_hi)
    rowi = lax.broadcasted_iota(I32, gate.shape, 0)
    row_f = rowi.astype(F32)
    g = jnp.where(rowi < own, gate, -jnp.inf)
    sel = jnp.zeros(gate.shape, F32)
    for _ in range(MOBA_TOPK):
        m = jnp.max(g, axis=0, keepdims=True)
        idx = jnp.min(jnp.where(g == m, row_f, 1e9), axis=0, keepdims=True)
        pick = row_f == idx
        sel = jnp.where(pick, 1.0, sel)
        g = jnp.where(pick, -jnp.inf, g)
    sel = jnp.where(rowi < own, sel, 0.0)
    mq_t = jnp.where(jnp.where(rowi < own - 1, sel, 0.0) > 0.0, 0.0, -MASK_BIG).astype(BF16)
    qs_t = (q_t * (hd ** -0.5 * LOG2E)).astype(BF16)
    q_aug_t = jnp.concatenate([qs_t, mq_t], axis=0)

    prev = jnp.maximum(own - 1, 0)
    st_own = pl.multiple_of(own * MOBA_BLOCK, MOBA_BLOCK)
    st_prev = pl.multiple_of(prev * MOBA_BLOCK, MOBA_BLOCK)
    s_own = dotf(ka_ref[pl.ds(st_own, MOBA_BLOCK), 0:hd], qs_t) + ob_ref[0]
    sel_prev = jnp.max(jnp.where(rowi == own - 1, sel, 0.0), axis=0, keepdims=True)
    s_prev = dotf(ka_ref[pl.ds(st_prev, MOBA_BLOCK), 0:hd], qs_t) + pb_ref[0] \
        + jnp.where(sel_prev > 0.0, 0.0, -MASK_BIG)
    m0 = jnp.maximum(jnp.max(s_own, axis=0, keepdims=True), jnp.max(s_prev, axis=0, keepdims=True))
    p_own = jnp.exp2(s_own - m0)
    p_prev = jnp.exp2(s_prev - m0)
    l0 = jnp.sum(p_own, axis=0, keepdims=True) + jnp.sum(p_prev, axis=0, keepdims=True)
    acc0 = dotf(vt_ref[:, pl.ds(st_own, MOBA_BLOCK)], p_own.astype(BF16)) \
        + dotf(vt_ref[:, pl.ds(st_prev, MOBA_BLOCK)], p_prev.astype(BF16))

    span = MOBA_GROUP * MOBA_BLOCK

    n_groups = (jnp.maximum(own - 1, 0) + MOBA_GROUP - 1) // MOBA_GROUP

    def scores(g):
        st = pl.multiple_of(g * span, span)
        return dotf(ka_ref[pl.ds(st, span), :], q_aug_t)

    s_buf[0] = scores(0)

    def group(it, carry):
        m, l, acc = carry
        cur = it % 2
        s_buf[1 - cur] = scores(jnp.minimum(it + 1, n_groups - 1))
        st = pl.multiple_of(it * span, span)
        s = s_buf[cur]
        m_new = jnp.maximum(m, jnp.max(s, axis=0, keepdims=True))
        p = jnp.exp2(s - m_new)
        alpha = jnp.exp2(m - m_new)
        l = alpha * l + jnp.sum(p, axis=0, keepdims=True)
        acc = alpha * acc + dotf(vt_ref[:, pl.ds(st, span)], p.astype(BF16))
        return m_new, l, acc

    _, l, acc = lax.fori_loop(0, n_groups, group, (m0, l0, acc0))
    o_ref[...] = (acc * (1.0 / l)).T.astype(o_ref.dtype)


def _moba_prompt(proj, kaug, v_t, kmean_pad, own_bias, prev_bias):
    t = proj.shape[0]
    hd = HEAD_DIM
    nb = t // MOBA_BLOCK
    assert nb % MOBA_GROUP == 0 and nb <= LANES
    return pl.pallas_call(
        _moba_body, grid=(N_HEADS, nb),
        in_specs=[pl.BlockSpec((MOBA_BLOCK, hd), lambda h, c: (c, h)),
                  pl.BlockSpec((t, 2 * hd), lambda h, c: (0, h)),
                  pl.BlockSpec((hd, t), lambda h, c: (h, 0)),
                  pl.BlockSpec((LANES, hd), lambda h, c: (0, h)),
                  pl.BlockSpec((1, MOBA_BLOCK, MOBA_BLOCK), lambda h, c: (h, 0, 0)),
                  pl.BlockSpec((1, MOBA_BLOCK, MOBA_BLOCK), lambda h, c: (h, 0, 0))],
        out_specs=pl.BlockSpec((MOBA_BLOCK, hd), lambda h, c: (c, h)),
        out_shape=jax.ShapeDtypeStruct((t, ATT_WIDTH), BF16),
        scratch_shapes=[pltpu.VMEM((2, MOBA_GROUP * MOBA_BLOCK, MOBA_BLOCK), F32)],
        compiler_params=_cparams(("arbitrary", "arbitrary"), 56), name="moba_prompt")(
            proj, kaug, v_t, kmean_pad, own_bias, prev_bias)


def _ssd_body(z_ref, x_ref, b_ref, c_ref, dt_ref, cw_ref, cb_ref, dtb_ref, alog_ref, dsk_ref, nrm_ref,
              exp_ref, expt_ref, y_ref, st_ref, tail_ref, xpad, state):
    i = pl.program_id(0)
    L = SSD_CHUNK
    W = SSD_WIDTH
    GN = N_SSD_GROUPS * SSD_STATE

    @pl.when(i == 0)
    def _():
        xpad[0:8, :] = jnp.zeros((8, CONV_DIM), F32)
        state[...] = jnp.zeros(state.shape, F32)

    xpad[8:8 + L, 0:W] = x_ref[...]
    xpad[8:8 + L, W:W + GN] = b_ref[...]
    xpad[8:8 + L, W + GN:W + 2 * GN] = c_ref[...]
    conv = cb_ref[...] + xpad[5:5 + L, :] * cw_ref[0:1, :]
    for r in range(1, CONV_WIDTH):
        conv = conv + xpad[5 + r:5 + r + L, :] * cw_ref[r:r + 1, :]
    tail = xpad[L:L + 8, :]
    tail_ref[...] = tail
    xpad[0:8, :] = tail
    act = _silu(conv)
    xs = act[:, 0:W]
    bm = act[:, W:W + GN].astype(BF16)
    cm = act[:, W + GN:W + 2 * GN].astype(BF16)

    dtv = dt_ref[...] + dtb_ref[...]
    dt = jnp.maximum(dtv, 0.0) + jnp.log1p(jnp.exp(-jnp.abs(dtv)))
    a = -jnp.exp(alog_ref[...])
    adt = dt * a
    row = lax.broadcasted_iota(I32, (L, L), 0)
    col = lax.broadcasted_iota(I32, (L, L), 1)
    causal = row >= col
    tril = jnp.where(causal, 1.0, 0.0).astype(BF16)
    dotf = functools.partial(jnp.dot, preferred_element_type=F32)
    acum = sum(dotf(tril, part) for part in _split3(adt))
    acum_t = acum.T
    expand = exp_ref[...]

    def widen(v):
        return sum(dotf(part, expand) for part in _split3(v))

    xd = xs * widen(dt)
    xd_b = xd.astype(BF16)
    lane = lax.broadcasted_iota(I32, (L, LANES), 1)
    low_half = lane < SSD_HEAD_DIM
    dg = functools.partial(lax.dot_general, dimension_numbers=NT_DIMS, preferred_element_type=F32)
    hpg = N_SSD_HEADS // N_SSD_GROUPS
    y_cols = []
    cbs = [dg(cm[:, g * SSD_STATE:(g + 1) * SSD_STATE], bm[:, g * SSD_STATE:(g + 1) * SSD_STATE])
           for g in range(N_SSD_GROUPS)]
    for pair in range(N_SSD_HEADS // 2):
        cb = cbs[(2 * pair) // hpg]
        xj = xd_b[:, pair * LANES:(pair + 1) * LANES]
        acc = None
        for sub in range(2):
            h = 2 * pair + sub
            diff = acum[:, h:h + 1] - acum_t[h:h + 1, :]
            dec = jnp.exp(jnp.where(causal, diff, -jnp.inf))
            mh = (cb * dec).astype(BF16)
            xh = jnp.where(low_half if sub == 0 else jnp.logical_not(low_half), xj, jnp.zeros_like(xj))
            part = dotf(mh, xh)
            acc = part if acc is None else acc + part
        y_cols.append(acc)
    y_diag = jnp.concatenate(y_cols, axis=1)

    s_prev = state[...]
    s_b = s_prev.astype(BF16)
    half_w = W // N_SSD_GROUPS
    y_off = jnp.concatenate(
        [dg(cm[:, g * SSD_STATE:(g + 1) * SSD_STATE], s_b[g * half_w:(g + 1) * half_w, :])
         for g in range(N_SSD_GROUPS)], axis=1) * widen(jnp.exp(acum))

    dte = jnp.exp(acum[L - 1:L, :] - acum)
    xdd = xd * widen(dte)
    upd = jnp.concatenate(
        [dotf(xdd[:, g * half_w:(g + 1) * half_w].T.astype(BF16), bm[:, g * SSD_STATE:(g + 1) * SSD_STATE])
         for g in range(N_SSD_GROUPS)], axis=0)
    last = jnp.broadcast_to(acum_t[:, L - 1:L], (LANES, SSD_STATE))
    dec_rows = jnp.exp(sum(dotf(expt_ref[...], part) for part in _split3(last)))
    s_new = s_prev * dec_rows + upd
    state[...] = s_new
    st_ref[...] = s_new

    y = y_diag + y_off + dsk_ref[...] * xs
    y = y * _silu(z_ref[...])
    r = lax.rsqrt(jnp.mean(y * y, axis=-1, keepdims=True) + RMS_EPS)
    y_ref[...] = ((y * r) * nrm_ref[...]).astype(y_ref.dtype)


def _head_expand():
    h = jnp.arange(LANES)[:, None]
    ch = jnp.arange(SSD_WIDTH)[None, :]
    return jnp.where(ch // SSD_HEAD_DIM == h, 1.0, 0.0).astype(BF16)


def _ssd_prompt(proj, conv_w, conv_b, dt_bias_p, a_log_p, dskip_w, ssd_norm, expand, expand_t):
    t = proj.shape[0]
    L = SSD_CHUNK
    W = SSD_WIDTH
    GN = N_SSD_GROUPS * SSD_STATE
    const = lambda shape: pl.BlockSpec(shape, lambda i: (0, 0))
    return pl.pallas_call(
        _ssd_body, grid=(t // L,),
        in_specs=[pl.BlockSpec((L, W), lambda i: (i, COL_Z // W)),
                  pl.BlockSpec((L, W), lambda i: (i, COL_X // W)),
                  pl.BlockSpec((L, GN), lambda i: (i, COL_B // GN)),
                  pl.BlockSpec((L, GN), lambda i: (i, COL_C // GN)),
                  pl.BlockSpec((L, LANES), lambda i: (i, COL_DT // LANES)),
                  const((CONV_WIDTH, CONV_DIM)), const((1, CONV_DIM)), const((1, LANES)), const((1, LANES)),
                  const((1, W)), const((1, W)), const((LANES, W)), const((W, LANES))],
        out_specs=[pl.BlockSpec((L, W), lambda i: (i, 0)),
                   pl.BlockSpec((W, SSD_STATE), lambda i: (0, 0)),
                   pl.BlockSpec((8, CONV_DIM), lambda i: (0, 0))],
        out_shape=[jax.ShapeDtypeStruct((t, W), BF16),
                   jax.ShapeDtypeStruct((W, SSD_STATE), F32),
                   jax.ShapeDtypeStruct((8, CONV_DIM), F32)],
        scratch_shapes=[pltpu.VMEM((L + 8, CONV_DIM), F32), pltpu.VMEM((W, SSD_STATE), F32)],
        compiler_params=_cparams(("arbitrary",), 40), name="ssd_prompt")(
            proj, proj, proj, proj, proj, conv_w, conv_b, dt_bias_p, a_log_p, dskip_w, ssd_norm, expand, expand_t)


def _outproj_body(a_ref, s_ref, w_ref, x_ref, g_ref, o_ref):
    w = ATT_WIDTH
    acc = jnp.dot(a_ref[...], w_ref[0:w, :], preferred_element_type=F32)
    acc = acc + jnp.dot(s_ref[...], w_ref[w:, :], preferred_element_type=F32)
    o_ref[...] = x_ref[...] + g_ref[...] * acc


def _outproj(att, ssd, w_out_b, x, mod, tm):
    t, d = x.shape
    r = mod.shape[0]
    tn = PROJ_TN
    return pl.pallas_call(
        _outproj_body, grid=(t // tm, d // tn),
        in_specs=[pl.BlockSpec((tm, ATT_WIDTH), lambda i, j: (i, 0)),
                  pl.BlockSpec((tm, SSD_WIDTH), lambda i, j: (i, 0)),
                  pl.BlockSpec((ATT_WIDTH + SSD_WIDTH, tn), lambda i, j: (0, j)),
                  pl.BlockSpec((tm, tn), lambda i, j: (i, j)),
                  pl.BlockSpec((r, tn), lambda i, j: (0, 2 * (d // tn) + j))],
        out_specs=pl.BlockSpec((tm, tn), lambda i, j: (i, j)),
        out_shape=jax.ShapeDtypeStruct((t, d), F32),
        compiler_params=_cparams(("arbitrary", "arbitrary"), 40), name="outproj")(att, ssd, w_out_b, x, mod)


def _group_allreduce(v, lane, op):
    for s in (1, 2, 4):
        up = pltpu.roll(v, LANES - s, axis=1)
        dn = pltpu.roll(v, s, axis=1)
        v = op(v, jnp.where((lane & s) == 0, up, dn))
    return v


def _store_token_tiles(ref, packed, n_tok):
    for s in range(TILE_ROWS):
        ref[pl.ds(s, n_tok, stride=TILE_ROWS), :] = packed[:, s * LANES:(s + 1) * LANES]


def _load_token_tiles(ref, n_tok, start=0, stride=TILE_ROWS):
    return jnp.concatenate([ref[pl.ds(start + s, n_tok, stride=stride), :] for s in range(TILE_ROWS)], axis=1)


def _ffnpre_body(x_ref, g_ref, sh_ref, sc_ref, wrh_ref, wrl_ref, rb_ref, cin_ref,
                 hp_ref, idx_ref, wt_ref, rank_ref, cnt_ref, cnt_scr, *, rows, n_real):
    i = pl.program_id(0)
    real = i < n_real

    @pl.when(i == 0)
    def _():
        cnt_scr[...] = cin_ref[...]

    h = _norm_mod(x_ref[...], g_ref[...], sh_ref[...], sc_ref[...])
    half = D_MODEL // 2
    packed = _pack_pair(h[:, :half], h[:, half:])
    _store_token_tiles(hp_ref, jnp.where(real, packed, jnp.zeros_like(packed)), rows)
    if rows * TILE_ROWS != hp_ref.shape[0]:
        hp_ref[rows * TILE_ROWS:, :] = jnp.zeros((hp_ref.shape[0] - rows * TILE_ROWS, LANES), U32)
    h_hi, h_lo = _split2(h)
    dotf = functools.partial(jnp.dot, preferred_element_type=F32)
    logits = dotf(h_hi, wrh_ref[...]) + dotf(h_hi, wrl_ref[...]) + dotf(h_lo, wrh_ref[...])
    scores = jax.nn.sigmoid(logits)
    biased = scores + rb_ref[...]
    lane = lax.broadcasted_iota(I32, biased.shape, 1)
    lane_f = lane.astype(F32)
    per_group = N_EXPERTS // N_EXPERT_GROUPS
    m1 = _group_allreduce(biased, lane, jnp.maximum)
    i1 = _group_allreduce(jnp.where(biased == m1, lane_f, 1e9), lane, jnp.minimum)
    m2 = _group_allreduce(jnp.where(lane_f == i1, -jnp.inf, biased), lane, jnp.maximum)
    gs = m1 + m2
    grp = (lane // per_group) % N_EXPERT_GROUPS
    beaten = jnp.zeros(biased.shape, F32)
    for k in range(1, N_EXPERT_GROUPS):
        other = pltpu.roll(gs, per_group * k, axis=1)
        other_grp = (grp - k) % N_EXPERT_GROUPS
        wins = jnp.where(other > gs, 1.0, jnp.where(other == gs, jnp.where(other_grp < grp, 1.0, 0.0), 0.0))
        beaten = beaten + wins
    keep = jnp.where(lane < N_EXPERTS, jnp.where(beaten < TOPK_GROUPS, 1.0, 0.0), 0.0)
    masked = jnp.where(keep > 0.0, biased, -jnp.inf)
    idx_out = jnp.zeros(biased.shape, F32)
    w_out = jnp.zeros(biased.shape, F32)
    w_sum = jnp.zeros((biased.shape[0], 1), F32)
    member = jnp.zeros(biased.shape, F32)
    picks = []
    for k in range(TOP_K):
        m = jnp.max(masked, axis=1, keepdims=True)
        ik = jnp.min(jnp.where(masked == m, lane_f, 1e9), axis=1, keepdims=True)
        pick = lane_f == ik
        wk = jnp.sum(jnp.where(pick, scores, 0.0), axis=1, keepdims=True)
        masked = jnp.where(pick, -jnp.inf, masked)
        member = jnp.where(pick, 1.0, member)
        picks.append(pick)
        idx_out = jnp.where(lane == k, ik, idx_out)
        w_out = jnp.where(lane == k, wk, w_out)
        w_sum = w_sum + wk
    member = jnp.where(real, member, 0.0)
    ri = lax.broadcasted_iota(I32, (rows, rows), 0)
    ci = lax.broadcasted_iota(I32, (rows, rows), 1)
    before = jnp.where(ri > ci, 1.0, 0.0).astype(BF16)
    pos = cnt_scr[...] + dotf(before, member.astype(BF16))
    rank_out = jnp.zeros(biased.shape, F32)
    for k in range(TOP_K):
        rk = jnp.sum(jnp.where(picks[k], pos, 0.0), axis=1, keepdims=True)
        rank_out = jnp.where(lane == k, rk, rank_out)

    @pl.when(real)
    def _():
        idx_ref[...] = idx_out.astype(I32)
        wt_ref[...] = w_out / w_sum * ROUTED_SCALE
        rank_ref[...] = rank_out.astype(I32)

    total = cnt_scr[...] + jnp.sum(member, axis=0, keepdims=True)
    cnt_scr[...] = total
    cnt_ref[...] = jnp.broadcast_to(total, cnt_ref.shape)


def _ffnpre(x, g, mod, wr_hi, wr_lo, rb, cnt_in, tm, hp_prev=None, hp_tok=None):
    t, d = x.shape
    r = mod.shape[0]
    n_real = t // tm
    steps = n_real if hp_prev is not None else hp_tok // tm
    row_map = lambda i: (jnp.minimum(i, n_real - 1), 0)
    const = lambda i: (0, 0)
    in_specs = [pl.BlockSpec((tm, d), row_map),
                pl.BlockSpec((1, d), const),
                pl.BlockSpec((r, d), lambda i: (0, 3)),
                pl.BlockSpec((r, d), lambda i: (0, 4)),
                pl.BlockSpec((d, LANES), const),
                pl.BlockSpec((d, LANES), const),
                pl.BlockSpec((1, LANES), const),
                pl.BlockSpec((1, LANES), const)]
    args = [x, g, mod, mod, wr_hi, wr_lo, rb, cnt_in]
    small = [pl.BlockSpec((tm, LANES), row_map), pl.BlockSpec((tm, LANES), row_map),
             pl.BlockSpec((tm, LANES), row_map), pl.BlockSpec((TILE_ROWS, LANES), const)]
    small_shape = [jax.ShapeDtypeStruct((t, LANES), I32), jax.ShapeDtypeStruct((t, LANES), F32),
                   jax.ShapeDtypeStruct((t, LANES), I32), jax.ShapeDtypeStruct((TILE_ROWS, LANES), F32)]
    inner = functools.partial(_ffnpre_body, rows=tm, n_real=n_real)
    n_in = len(args)
    if hp_prev is None:
        out_specs = [pl.BlockSpec((tm * TILE_ROWS, LANES), lambda i: (i, 0))] + small
        out_shape = [jax.ShapeDtypeStruct((hp_tok * TILE_ROWS, LANES), U32)] + small_shape
        body = inner
        aliases = {}
    else:
        start, blk = hp_tok
        in_specs.append(pl.BlockSpec(memory_space=pl.ANY))
        args.append(hp_prev)
        out_specs = [pl.BlockSpec((blk * TILE_ROWS, LANES), lambda i: (start // blk, 0))] + small
        out_shape = [jax.ShapeDtypeStruct(hp_prev.shape, U32)] + small_shape
        body = lambda *refs: inner(*refs[:n_in], *refs[n_in + 1:])
        aliases = {n_in: 0}
    return pl.pallas_call(
        body, grid=(steps,), in_specs=in_specs, out_specs=out_specs, out_shape=out_shape,
        input_output_aliases=aliases, scratch_shapes=[pltpu.VMEM((1, LANES), F32)],
        compiler_params=_cparams(("arbitrary",), 40), name="ffnpre")(*args)


def _tile_rows(r):
    return pl.ds(pl.multiple_of(r * TILE_ROWS, TILE_ROWS), TILE_ROWS)


def _route_body(dest_ref, pstart_ref, cnt_ref, hp_ref, xs_ref, ztile, sem, zsem, *, tm, n_rows):
    i = pl.program_id(0)

    def zero_rows(lo, hi):
        def zcopy(r):
            return pltpu.make_async_copy(ztile, xs_ref.at[_tile_rows(r)], zsem.at[0])

        def start(r, carry):
            zcopy(r).start()
            return carry

        def wait(r, carry):
            zcopy(r).wait()
            return carry
        lax.fori_loop(lo, hi, start, 0)
        lax.fori_loop(lo, hi, wait, 0)

    def padded_end(e):
        cnt = cnt_ref[e]
        return pstart_ref[e] + (((cnt + MOE_ROWS - 1) // MOE_ROWS) * MOE_ROWS)

    @pl.when(i == 0)
    def _():
        ztile[...] = jnp.zeros(ztile.shape, U32)

        def per_expert(e, carry):
            zero_rows(pstart_ref[e] + cnt_ref[e], padded_end(e))
            return carry
        lax.fori_loop(0, N_EXPERTS, per_expert, 0)
        zero_rows(padded_end(N_EXPERTS - 1), n_rows)

    per_tile = tm * TOP_K

    def copy(r):
        row = dest_ref[i * per_tile + r]
        return pltpu.make_async_copy(hp_ref.at[_tile_rows(r // TOP_K)], xs_ref.at[_tile_rows(row)], sem.at[0])

    def start(r, carry):
        copy(r).start()
        return carry

    def wait(r, carry):
        copy(r).wait()
        return carry
    lax.fori_loop(0, per_tile, start, 0, unroll=8)
    lax.fori_loop(0, per_tile, wait, 0, unroll=8)


def _route(dest, pstart, counts, hp, n_tok, tm, n_rows):
    assert n_tok % tm == 0
    grid_spec = pltpu.PrefetchScalarGridSpec(
        num_scalar_prefetch=3, grid=(n_tok // tm,),
        in_specs=[pl.BlockSpec((tm * TILE_ROWS, LANES), lambda i, cd, ps, cn: (i, 0))],
        out_specs=pl.BlockSpec(memory_space=pl.ANY),
        scratch_shapes=[pltpu.VMEM((TILE_ROWS, LANES), U32),
                        pltpu.SemaphoreType.DMA((1,)), pltpu.SemaphoreType.DMA((1,))])
    return pl.pallas_call(
        functools.partial(_route_body, tm=tm, n_rows=n_rows), grid_spec=grid_spec,
        out_shape=jax.ShapeDtypeStruct((n_rows * TILE_ROWS, LANES), U32),
        compiler_params=_cparams(("arbitrary",), 32), name="route")(dest, pstart, counts, hp)


def _moe_body(blk_e_ref, nlive_ref, xs_ref, wg_ref, wu_ref, wd_ref, y_ref, wg_b, wu_b, wd_b):
    b = pl.program_id(0)
    live = b < nlive_ref[0]
    e = blk_e_ref[b]
    e_prev = blk_e_ref[jnp.maximum(b - 1, 0)]

    @pl.when(jnp.logical_and(live, jnp.logical_or(b == 0, e != e_prev)))
    def _():
        wg_b[...] = wg_ref[0].astype(BF16)
        wu_b[...] = wu_ref[0].astype(BF16)
        wd_b[...] = wd_ref[0].astype(BF16)

    @pl.when(live)
    def _():
        lo, hi = _unpack_pair(_load_token_tiles(xs_ref, MOE_ROWS))
        x = jnp.concatenate([lo, hi], axis=1).astype(BF16)
        dotf = functools.partial(jnp.dot, preferred_element_type=F32)
        act = (_silu(dotf(x, wg_b[...])) * dotf(x, wu_b[...])).astype(BF16)
        y = dotf(act, wd_b[...])
        half = D_MODEL // 2
        _store_token_tiles(y_ref, _pack_pair(y[:, :half], y[:, half:]), MOE_ROWS)

    @pl.when(jnp.logical_not(live))
    def _():
        y_ref[...] = jnp.zeros(y_ref.shape, U32)


def _moe(blk_e, n_live, xs, wg, wu, wd):
    n_blk = blk_e.shape[0]
    blk_rows = MOE_ROWS * TILE_ROWS
    grid_spec = pltpu.PrefetchScalarGridSpec(
        num_scalar_prefetch=2, grid=(n_blk,),
        in_specs=[pl.BlockSpec((blk_rows, LANES), lambda b, be, nl: (b, 0)),
                  pl.BlockSpec((1, D_MODEL, D_EXPERT), lambda b, be, nl: (be[b], 0, 0)),
                  pl.BlockSpec((1, D_MODEL, D_EXPERT), lambda b, be, nl: (be[b], 0, 0)),
                  pl.BlockSpec((1, D_EXPERT, D_MODEL), lambda b, be, nl: (be[b], 0, 0))],
        out_specs=pl.BlockSpec((blk_rows, LANES), lambda b, be, nl: (b, 0)),
        scratch_shapes=[pltpu.VMEM((D_MODEL, D_EXPERT), BF16), pltpu.VMEM((D_MODEL, D_EXPERT), BF16),
                        pltpu.VMEM((D_EXPERT, D_MODEL), BF16)])
    return pl.pallas_call(
        _moe_body, grid_spec=grid_spec,
        out_shape=jax.ShapeDtypeStruct(xs.shape, U32),
        compiler_params=_cparams(("arbitrary",), 48), name="moe")(blk_e, n_live, xs, wg, wu, wd)


def _expert_layout(cnt_total, n_assign):
    assert n_assign % MOE_ROWS == 0
    counts = cnt_total[0, :N_EXPERTS].astype(I32)
    padded = (counts + MOE_ROWS - 1) // MOE_ROWS * MOE_ROWS
    pends = jnp.cumsum(padded)
    pstart = (pends - padded).astype(I32)
    n_blk = n_assign // MOE_ROWS + N_EXPERTS
    blk_start = jnp.arange(n_blk, dtype=I32) * MOE_ROWS
    blk_e = jnp.minimum(jnp.sum((pends[None, :] <= blk_start[:, None]).astype(I32), axis=1), N_EXPERTS - 1)
    n_live = (pends[-1] // MOE_ROWS).astype(I32).reshape(1)
    return blk_e.astype(I32), n_live, pstart, counts, n_blk * MOE_ROWS


def _combine_body(dest_ref, x_ref, y_ref, hp_ref, wt_ref, wgs_ref, wus_ref, wds_ref, g_ref, nf_ref,
                  o_ref, gbuf, sem, *, tm, tok0):
    i = pl.program_id(0)
    n = pl.num_programs(0)
    slot = i % 2
    per_tile = tm * TOP_K

    def copy(j, r, sl):
        row = dest_ref[(tok0 + j * tm) * TOP_K + r]
        return pltpu.make_async_copy(y_ref.at[_tile_rows(row)], gbuf.at[sl, _tile_rows(r)], sem.at[sl])

    def for_assignments(fn):
        def step(r, carry):
            fn(r)
            return carry
        lax.fori_loop(0, per_tile, step, 0, unroll=8)

    @pl.when(i == 0)
    def _():
        for_assignments(lambda r: copy(0, r, 0).start())

    @pl.when(i + 1 < n)
    def _():
        for_assignments(lambda r: copy(i + 1, r, 1 - slot).start())

    for_assignments(lambda r: copy(i, r, slot).wait())
    wts = wt_ref[...]
    acc_lo = [None] * TILE_ROWS
    acc_hi = [None] * TILE_ROWS
    stride = TOP_K * TILE_ROWS
    for k in range(TOP_K):
        wk = wts[:, k:k + 1]
        for s in range(TILE_ROWS):
            l_k, h_k = _unpack_pair(gbuf[slot, pl.ds(k * TILE_ROWS + s, tm, stride=stride), :])
            acc_lo[s] = l_k * wk if acc_lo[s] is None else acc_lo[s] + l_k * wk
            acc_hi[s] = h_k * wk if acc_hi[s] is None else acc_hi[s] + h_k * wk
    routed = jnp.concatenate(acc_lo + acc_hi, axis=1)
    hl, hh = _unpack_pair(_load_token_tiles(hp_ref, tm))
    h = jnp.concatenate([hl, hh], axis=1).astype(BF16)
    dotf = functools.partial(jnp.dot, preferred_element_type=F32)
    act = (_silu(dotf(h, wgs_ref[...])) * dotf(h, wus_ref[...])).astype(BF16)
    shared = dotf(act, wds_ref[...])
    xo = x_ref[...] + g_ref[...] * (routed + shared)
    r = lax.rsqrt(jnp.mean(xo * xo, axis=-1, keepdims=True) + RMS_EPS)
    o_ref[...] = (xo * r) * nf_ref[...]


def _combine(dest, x1, ys, hp, wts, wgs, wus, wds, mod, norm_final, tm, tok0):
    t, d = x1.shape
    r = mod.shape[0]
    o = tok0 // tm
    const = lambda i, ds: (0, 0)
    grid_spec = pltpu.PrefetchScalarGridSpec(
        num_scalar_prefetch=1, grid=(t // tm,),
        in_specs=[pl.BlockSpec((tm, d), lambda i, ds: (i, 0)),
                  pl.BlockSpec(memory_space=pl.ANY),
                  pl.BlockSpec((tm * TILE_ROWS, LANES), lambda i, ds: (i + o, 0)),
                  pl.BlockSpec((tm, LANES), lambda i, ds: (i, 0)),
                  pl.BlockSpec((d, D_EXPERT), const),
                  pl.BlockSpec((d, D_EXPERT), const),
                  pl.BlockSpec((D_EXPERT, d), const),
                  pl.BlockSpec((r, d), lambda i, ds: (0, 5)),
                  pl.BlockSpec((1, d), const)],
        out_specs=pl.BlockSpec((tm, d), lambda i, ds: (i, 0)),
        scratch_shapes=[pltpu.VMEM((2, tm * TOP_K * TILE_ROWS, LANES), U32), pltpu.SemaphoreType.DMA((2,))])
    return pl.pallas_call(
        functools.partial(_combine_body, tm=tm, tok0=tok0), grid_spec=grid_spec,
        out_shape=jax.ShapeDtypeStruct((t, d), F32),
        compiler_params=_cparams(("arbitrary",), 48), name="combine")(
            dest, x1, ys, hp, wts, wgs, wus, wds, mod, norm_final)


PAGES_PER_STEP = 16


def _pagesum_body(pt_ref, *refs):
    ins = refs[:PAGES_PER_STEP]
    o_ref = refs[PAGES_PER_STEP]
    ppb = MOBA_BLOCK // PAGE_SIZE
    for j in range(PAGES_PER_STEP // ppb):
        acc = None
        for p in range(ppb):
            s = jnp.sum(ins[j * ppb + p][0].reshape(PAGE_SIZE, N_HEADS, HEAD_DIM), axis=0)
            acc = s if acc is None else acc + s
        for h in range(N_HEADS):
            o_ref[0, j:j + 1, h * HEAD_DIM:(h + 1) * HEAD_DIM] = acc[h:h + 1, :]


def _pagesum(page_table_flat, kpool, n_batch, n_pages):
    w = N_HEADS * HEAD_DIM
    ppb = MOBA_BLOCK // PAGE_SIZE
    steps = n_pages // PAGES_PER_STEP

    def in_map(p):
        return lambda b, j, pt: (pt[b * n_pages + j * PAGES_PER_STEP + p], 0, 0)

    grid_spec = pltpu.PrefetchScalarGridSpec(
        num_scalar_prefetch=1, grid=(n_batch, steps),
        in_specs=[pl.BlockSpec((1, PAGE_SIZE * N_HEADS, HEAD_DIM), in_map(p)) for p in range(PAGES_PER_STEP)],
        out_specs=pl.BlockSpec((1, PAGES_PER_STEP // ppb, w), lambda b, j, pt: (b, j, 0)))
    return pl.pallas_call(
        _pagesum_body, grid_spec=grid_spec,
        out_shape=jax.ShapeDtypeStruct((n_batch, n_pages // ppb, w), F32),
        compiler_params=_cparams(("arbitrary", "arbitrary"), 40), name="pagesum")(
            page_table_flat, *([kpool] * PAGES_PER_STEP))


def _gate3_body(q_ref, ks_ref, sel_ref):
    prod = ks_ref[0] * q_ref[0] * (1.0 / MOBA_BLOCK)
    g = jnp.concatenate([jnp.sum(prod[:, h * HEAD_DIM:(h + 1) * HEAD_DIM], axis=1, keepdims=True)
                         for h in range(N_HEADS)], axis=1)
    rowi = lax.broadcasted_iota(I32, g.shape, 0).astype(F32)
    picks = []
    for _ in range(MOBA_TOPK):
        m = jnp.max(g, axis=0, keepdims=True)
        ik = jnp.min(jnp.where(g == m, rowi, 1e9), axis=0, keepdims=True)
        g = jnp.where(rowi == ik, -jnp.inf, g)
        picks.append(ik)
    picks.append(jnp.zeros((8 - MOBA_TOPK, N_HEADS), F32))
    sel_ref[0] = jnp.concatenate(picks, axis=0).astype(I32)


def _gate3(q3, ksum):
    n_batch, nb, w = ksum.shape
    return pl.pallas_call(
        _gate3_body, grid=(n_batch,),
        in_specs=[pl.BlockSpec((1, 1, w), lambda b: (b, 0, 0)),
                  pl.BlockSpec((1, nb, w), lambda b: (b, 0, 0))],
        out_specs=pl.BlockSpec((1, 8, N_HEADS), lambda b: (b, 0, 0)),
        out_shape=jax.ShapeDtypeStruct((n_batch, 8, N_HEADS), I32),
        compiler_params=_cparams(("arbitrary",), 32), name="gate3")(q3, ksum)


def _decode_body(phys_ref, blk_ref, q_ref, kn_ref, vn_ref, *refs, last_block):
    ppb = MOBA_BLOCK // PAGE_SIZE
    n_pg = MOBA_TOPK * ppb
    k_refs = refs[:n_pg]
    v_refs = refs[n_pg:2 * n_pg]
    tab_ref = refs[2 * n_pg]
    o_ref = refs[2 * n_pg + 1]
    b = pl.program_id(0)
    h = pl.program_id(1)
    q = q_ref[0, 0]
    scale = HEAD_DIM ** -0.5
    qb = jnp.broadcast_to(q, (8, HEAD_DIM)).astype(BF16)
    dg = functools.partial(lax.dot_general, dimension_numbers=NT_DIMS, preferred_element_type=F32)
    s_self = jnp.sum(q * kn_ref[0, 0], axis=1, keepdims=True) * scale + tab_ref[0, 2:3, 0:1]
    scores = []
    vals = []
    head_rows = pl.ds(h, PAGE_SIZE, stride=N_HEADS)
    for s in range(MOBA_TOPK):
        kblk = jnp.concatenate([k_refs[s * ppb + p][0, head_rows, :] for p in range(ppb)], axis=0).astype(BF16)
        vals.append(jnp.concatenate([v_refs[s * ppb + p][0, head_rows, :] for p in range(ppb)], axis=0).astype(BF16))
        n = blk_ref[(b * N_HEADS + h) * MOBA_TOPK + s]
        bias = jnp.where(n == last_block, tab_ref[0, 0:1, :], tab_ref[0, 1:2, :])
        scores.append(dg(qb, kblk)[0:1, :] * scale + bias)
    m = s_self
    for sc in scores:
        m = jnp.maximum(m, jnp.max(sc, axis=1, keepdims=True))
    p_self = jnp.exp(s_self - m)
    l = p_self
    acc = p_self * vn_ref[0, 0]
    for sc, vblk in zip(scores, vals):
        p = jnp.exp(sc - m)
        l = l + jnp.sum(p, axis=1, keepdims=True)
        pb = jnp.broadcast_to(p, (8, MOBA_BLOCK)).astype(BF16)
        acc = acc + jnp.dot(pb, vblk, preferred_element_type=F32)[0:1, :]
    o_ref[0, 0] = acc * (1.0 / l)


def _decode(phys, blks, q4, kn4, vn4, kpool, vpool, tabs, last_block):
    n_batch = q4.shape[0]
    ppb = MOBA_BLOCK // PAGE_SIZE
    n_pg = MOBA_TOPK * ppb
    hd = HEAD_DIM

    def pg_map(i):
        return lambda b, h, ph, bl: (ph[(b * N_HEADS + h) * n_pg + i], 0, 0)

    vec = pl.BlockSpec((1, 1, 1, hd), lambda b, h, ph, bl: (b, h, 0, 0))
    page = (1, PAGE_SIZE * N_HEADS, hd)
    grid_spec = pltpu.PrefetchScalarGridSpec(
        num_scalar_prefetch=2, grid=(n_batch, N_HEADS),
        in_specs=[vec, vec, vec]
        + [pl.BlockSpec(page, pg_map(i)) for i in range(n_pg)]
        + [pl.BlockSpec(page, pg_map(i)) for i in range(n_pg)]
        + [pl.BlockSpec((1, 8, MOBA_BLOCK), lambda b, h, ph, bl: (h, 0, 0))],
        out_specs=vec)
    return pl.pallas_call(
        functools.partial(_decode_body, last_block=last_block), grid_spec=grid_spec,
        out_shape=jax.ShapeDtypeStruct((n_batch, N_HEADS, 1, hd), F32),
        compiler_params=_cparams(("arbitrary", "arbitrary"), 32), name="decode")(
            phys, blks, q4, kn4, vn4, *([kpool] * n_pg), *([vpool] * n_pg), tabs)


def _sconv_body(p_ref, cs_ref, cw_ref, cb_ref, dtb_ref, alog_ref, x_ref, b_ref, c_ref, dt_ref, da_ref, nc_ref):
    xbc = p_ref[:, COL_X:COL_X + CONV_DIM]
    cs = cs_ref[...]
    conv = cb_ref[...] + xbc * cw_ref[CONV_WIDTH - 1:CONV_WIDTH, :]
    for i in range(CONV_WIDTH - 1):
        conv = conv + cs[:, i * CONV_DIM:(i + 1) * CONV_DIM] * cw_ref[i:i + 1, :]
    nc_ref[:, 0:(CONV_WIDTH - 2) * CONV_DIM] = cs[:, CONV_DIM:]
    nc_ref[:, (CONV_WIDTH - 2) * CONV_DIM:] = xbc
    act = _silu(conv)
    gn = N_SSD_GROUPS * SSD_STATE
    x_ref[...] = act[:, :SSD_WIDTH]
    b_ref[...] = act[:, SSD_WIDTH:SSD_WIDTH + gn]
    c_ref[...] = act[:, SSD_WIDTH + gn:]
    dtv = p_ref[:, COL_DT:COL_DT + LANES] + dtb_ref[...]
    dt = jnp.maximum(dtv, 0.0) + jnp.log1p(jnp.exp(-jnp.abs(dtv)))
    dt_ref[...] = dt
    da_ref[...] = jnp.exp(dt * (-jnp.exp(alog_ref[...])))


def _sconv(proj_s, conv_state2, conv_w, conv_b, dt_bias_p, a_log_p):
    n = proj_s.shape[0]
    gn = N_SSD_GROUPS * SSD_STATE
    shapes = [(n, SSD_WIDTH), (n, gn), (n, gn), (n, LANES), (n, LANES), (n, (CONV_WIDTH - 1) * CONV_DIM)]
    return pl.pallas_call(
        _sconv_body,
        out_shape=[jax.ShapeDtypeStruct(s, F32) for s in shapes],
        compiler_params=pltpu.CompilerParams(vmem_limit_bytes=32 * 1024 * 1024), name="sconv")(
            proj_s, conv_state2, conv_w, conv_b, dt_bias_p, a_log_p)


def _sstate_body(st_ref, x_ref, dt_ref, da_ref, b_ref, c_ref, ns_ref, y_ref):
    hpg = N_SSD_HEADS // N_SSD_GROUPS
    for g in range(N_SSD_GROUPS):
        hs = slice(g * hpg, (g + 1) * hpg)
        st = st_ref[0, hs]
        xdt = x_ref[0, hs] * dt_ref[0, hs]
        ns = st * da_ref[0, hs] + xdt * b_ref[0, g]
        ns_ref[0, hs] = ns
        y_ref[0, hs] = jnp.sum(ns * c_ref[0, g], axis=-1, keepdims=True)


def _sstate(state, x4, dt4, da4, b4, c4):
    n = state.shape[0]
    h, p, s = N_SSD_HEADS, SSD_HEAD_DIM, SSD_STATE
    g = N_SSD_GROUPS
    m4 = lambda b: (b, 0, 0, 0)
    return pl.pallas_call(
        _sstate_body, grid=(n,),
        in_specs=[pl.BlockSpec((1, h, p, s), m4), pl.BlockSpec((1, h, p, 1), m4),
                  pl.BlockSpec((1, h, 1, 1), m4), pl.BlockSpec((1, h, 1, 1), m4),
                  pl.BlockSpec((1, g, 1, s), m4), pl.BlockSpec((1, g, 1, s), m4)],
        out_specs=[pl.BlockSpec((1, h, p, s), m4), pl.BlockSpec((1, h, p, 1), m4)],
        out_shape=[jax.ShapeDtypeStruct((n, h, p, s), F32), jax.ShapeDtypeStruct((n, h, p, 1), F32)],
        compiler_params=_cparams(("arbitrary",), 32), name="sstate")(state, x4, dt4, da4, b4, c4)


def _sgate_body(y_ref, x_ref, p_ref, dsk_ref, nrm_ref, o_ref):
    y = y_ref[...] + dsk_ref[...] * x_ref[...]
    y = y * _silu(p_ref[:, COL_Z:COL_Z + SSD_WIDTH])
    r = lax.rsqrt(jnp.mean(y * y, axis=-1, keepdims=True) + RMS_EPS)
    o_ref[...] = ((y * r) * nrm_ref[...]).astype(o_ref.dtype)


def _sgate(y, xs, proj_s, dskip_w, ssd_norm):
    return pl.pallas_call(
        _sgate_body, out_shape=jax.ShapeDtypeStruct(y.shape, BF16),
        compiler_params=pltpu.CompilerParams(vmem_limit_bytes=32 * 1024 * 1024), name="sgate")(
            y, xs, proj_s, dskip_w, ssd_norm)


def kernel(x_prompt, x_sample, cache_k, cache_v, state_ssm, state_conv, page_table, c_prompt, c_sample,
           w_ada, b_ada, norm_mix, norm_ffn, w_in, conv_w, conv_b, dt_bias, a_log, d_skip, ssd_norm,
           w_out, rel_bias, w_router, router_bias, w_gate_e, w_up_e, w_down_e, w_gate_s, w_up_s, w_down_s,
           norm_final):
    depth = w_ada.shape[0]
    assert depth == 1
    bp, sp, d = x_prompt.shape
    bs, sn, _ = x_sample.shape
    assert bp == 1 and sn == 1 and d == D_MODEL
    n_pages = page_table.shape[1]
    past_len = n_pages * PAGE_SIZE
    nb_prompt = sp // MOBA_BLOCK
    nb_past = past_len // MOBA_BLOCK
    l = 0
    layer = lambda a: a.reshape(a.shape[1:])

    w_in_b = jnp.pad(layer(w_in).astype(BF16), ((0, 0), (0, IN_COLS_PAD - IN_COLS)))
    w_out_b = layer(w_out).astype(BF16)
    wgs_b, wus_b, wds_b = (layer(w).astype(BF16) for w in (w_gate_s, w_up_s, w_down_s))
    wr2 = jnp.concatenate([layer(w_router), layer(w_router)], axis=1)
    wr_hi = wr2.astype(BF16)
    wr_lo = (wr2 - wr_hi.astype(F32)).astype(BF16)
    rb2 = jnp.concatenate([router_bias[l], router_bias[l]])[None, :].astype(F32)
    g_mix = norm_mix[l][None, :]
    g_ffn = norm_ffn[l][None, :]
    g_fin = norm_final[None, :]
    pad_h = LANES - N_SSD_HEADS
    dt_bias_p = jnp.pad(dt_bias[l].astype(F32), (0, pad_h))[None, :]
    a_log_p = jnp.pad(a_log[l].astype(F32), (0, pad_h))[None, :]
    dskip_w = jnp.repeat(d_skip[l].astype(F32), SSD_HEAD_DIM)[None, :]
    nrm_ssd = ssd_norm[l][None, :]
    cw = conv_w[l]
    cbias = conv_b[l][None, :]
    expand = _head_expand()
    expand_t = expand.T
    own_bias, prev_bias = _prompt_bias_tables(rel_bias)
    rel_t = rel_bias.T

    c_all = jnp.concatenate([c_prompt, jnp.zeros((8 - bp, d), F32), c_sample], axis=0)
    mod = _adaln(c_all, layer(w_ada), b_ada[l][None, :])
    mod_p = mod[0:1]
    mod_s = mod[8:8 + bs]

    xp = x_prompt.reshape(sp, d)
    proj = _inproj(xp, g_mix, mod_p, w_in_b, 1024)
    k_p, v_p, kaug, v_t, kmean = _kvpost(proj)
    kmean_pad = jnp.pad(kmean.reshape(nb_prompt, ATT_WIDTH), ((0, LANES - nb_prompt), (0, 0)))
    att = _moba_prompt(proj, kaug, v_t, kmean_pad, own_bias, prev_bias)
    ssd, ssm_p, conv_tail = _ssd_prompt(proj, cw, cbias, dt_bias_p, a_log_p, dskip_w, nrm_ssd, expand, expand_t)
    xp1 = _outproj(att, ssd, w_out_b, xp, mod_p, 1024)

    xs = x_sample.reshape(bs, d)
    proj_s = _inproj(xs, g_mix, mod_s, w_in_b, bs)
    q_s = proj_s[:, COL_Q:COL_Q + ATT_WIDTH]
    k_s = proj_s[:, COL_K:COL_K + ATT_WIDTH]
    v_s = proj_s[:, COL_V:COL_V + ATT_WIDTH]
    kpool = cache_k.reshape(cache_k.shape[1], PAGE_SIZE * N_HEADS, HEAD_DIM)
    vpool = cache_v.reshape(cache_v.shape[1], PAGE_SIZE * N_HEADS, HEAD_DIM)
    ksum = _pagesum(page_table.reshape(-1), kpool, bs, n_pages)
    sel = _gate3(q_s.reshape(bs, 1, ATT_WIDTH), ksum)
    blks = jnp.transpose(sel[:, :MOBA_TOPK, :], (0, 2, 1))
    ppb = MOBA_BLOCK // PAGE_SIZE
    lpage = blks[..., None] * ppb + jnp.arange(ppb, dtype=I32)
    phys = jnp.take_along_axis(page_table, lpage.reshape(bs, -1), axis=1)
    dist_last = past_len - ((nb_past - 1) * MOBA_BLOCK + jnp.arange(MOBA_BLOCK))
    tab_last = _bias_lookup(rel_t, _rel_bucket(dist_last))
    tab_far = jnp.broadcast_to(rel_t[:, NUM_BUCKETS - 1:NUM_BUCKETS], tab_last.shape)
    tab_self = jnp.broadcast_to(rel_t[:, 0:1], tab_last.shape)
    tabs = jnp.stack([tab_last, tab_far, tab_self] + [jnp.zeros_like(tab_last)] * 5, axis=1).astype(F32)
    shape4 = (bs, N_HEADS, 1, HEAD_DIM)
    att_s = _decode(phys.reshape(-1).astype(I32), blks.reshape(-1).astype(I32), q_s.reshape(shape4),
                    k_s.reshape(shape4), v_s.reshape(shape4), kpool, vpool, tabs, nb_past - 1)
    att_s = att_s.reshape(bs, ATT_WIDTH).astype(BF16)

    xs_act, b_s, c_s, dt_s, da_s, conv_s = _sconv(
        proj_s, state_conv.reshape(bs, (CONV_WIDTH - 1) * CONV_DIM), cw, cbias, dt_bias_p, a_log_p)
    ssm_s, y_s = _sstate(
        layer(state_ssm), xs_act.reshape(bs, N_SSD_HEADS, SSD_HEAD_DIM, 1),
        dt_s[:, :N_SSD_HEADS].reshape(bs, N_SSD_HEADS, 1, 1), da_s[:, :N_SSD_HEADS].reshape(bs, N_SSD_HEADS, 1, 1),
        b_s.reshape(bs, N_SSD_GROUPS, 1, SSD_STATE), c_s.reshape(bs, N_SSD_GROUPS, 1, SSD_STATE))
    ssd_s = _sgate(y_s.reshape(bs, SSD_WIDTH), xs_act, proj_s, dskip_w, nrm_ssd)
    xs1 = _outproj(att_s, ssd_s, w_out_b, xs, mod_s, bs)

    n_tok = sp + bs
    n_assign = n_tok * TOP_K
    ffn_tm = 256
    assert sp % ffn_tm == 0 and sp % bs == 0 and bs <= ffn_tm
    cnt0 = jnp.zeros((1, LANES), F32)
    hp, idx_p, wt_p, rank_p, cnt_p = _ffnpre(xp1, g_ffn, mod_p, wr_hi, wr_lo, rb2, cnt0, ffn_tm,
                                             hp_tok=sp + ffn_tm)
    hp, idx_s, wt_s, rank_s, cnt_all = _ffnpre(xs1, g_ffn, mod_s, wr_hi, wr_lo, rb2, cnt_p[0:1], bs,
                                               hp_prev=hp, hp_tok=(sp, bs))
    idx = jnp.concatenate([idx_p[:, :TOP_K], idx_s[:, :TOP_K]], axis=0)
    rank = jnp.concatenate([rank_p[:, :TOP_K], rank_s[:, :TOP_K]], axis=0)
    blk_e, n_live, pstart, counts, n_rows = _expert_layout(cnt_all, n_assign)
    experts = jnp.arange(N_EXPERTS, dtype=I32)
    dest = (rank + jnp.sum(jnp.where(idx[..., None] == experts, pstart, 0), axis=-1)).reshape(-1).astype(I32)
    x_sorted = _route(dest, pstart, counts, hp, n_tok, bs, n_rows)
    y_sorted = _moe(blk_e, n_live, x_sorted, layer(w_gate_e), layer(w_up_e), layer(w_down_e))
    y_p = _combine(dest, xp1, y_sorted, hp, wt_p, wgs_b, wus_b, wds_b, mod_p, g_fin, 128, 0)
    y_s_out = _combine(dest, xs1, y_sorted, hp, wt_s, wgs_b, wus_b, wds_b, mod_s, g_fin, bs, sp)

    kv_shape = (1, bp, sp, N_HEADS, HEAD_DIM)
    kvs_shape = (1, bs, sn, N_HEADS, HEAD_DIM)
    return (y_p.reshape(bp, sp, d), y_s_out.reshape(bs, sn, d),
            k_p.reshape(kv_shape), v_p.reshape(kv_shape),
            k_s.reshape(kvs_shape), v_s.reshape(kvs_shape),
            ssm_p.reshape(1, bp, N_SSD_HEADS, SSD_HEAD_DIM, SSD_STATE),
            ssm_s.reshape(1, bs, N_SSD_HEADS, SSD_HEAD_DIM, SSD_STATE),
            conv_tail[8 - (CONV_WIDTH - 1):].reshape(1, bp, CONV_WIDTH - 1, CONV_DIM),
            conv_s.reshape(1, bs, CONV_WIDTH - 1, CONV_DIM))
```
